```python
import math
import jax, jax.numpy as jnp
from jax import lax
import numpy as np

D_MODEL = 1024
BATCH = 4
SEQ = 4096
DEPTH = 4

N_MIXERS = 2
N_SSD = (DEPTH + 1) // 2
N_FOX = DEPTH // 2
EPS = 1e-6

SSD_EXPAND = 2
SSD_D_INNER = SSD_EXPAND * D_MODEL
SSD_HEAD_DIM = 64
SSD_HEADS = SSD_D_INNER // SSD_HEAD_DIM
SSD_GROUPS = 4
SSD_HPG = SSD_HEADS // SSD_GROUPS
SSD_STATE = 128
SSD_CONV = 4
SSD_CHUNK = 128
SSD_CONV_DIM = SSD_D_INNER + 2 * SSD_GROUPS * SSD_STATE
SSD_IN_DIM = SSD_D_INNER + SSD_CONV_DIM + SSD_HEADS

FOX_HEAD_DIM = 64
FOX_HEADS = D_MODEL // FOX_HEAD_DIM
FOX_D = FOX_HEADS * FOX_HEAD_DIM
FOX_IN_DIM = 4 * FOX_D + FOX_HEADS
Q_BLOCK = 128

D_FF = 2816
FFN_CONV = 3

kernel_name = 'hybrid_ssd_fox_convglu_trunk'


def rmsnorm(x, g):
    xf = x.astype(jnp.float32)
    y = xf * lax.rsqrt(jnp.mean(xf * xf, axis=-1, keepdims=True) + EPS)
    return (y * g.astype(jnp.float32)).astype(x.dtype)


def causal_dwconv(x, w, b):
    k_w, c = w.shape
    y = lax.conv_general_dilated(x, w[:, None, :].astype(x.dtype), window_strides=(1,),
                                 padding=[(k_w - 1, 0)],
                                 dimension_numbers=('NWC', 'WIO', 'NWC'),
                                 feature_group_count=c)
    return y + b.astype(x.dtype)


def ssd_chunked(xs, dt, a, bm, cm):
    bsz, s_len = xs.shape[0], xs.shape[1]
    nc, l_c = s_len // SSD_CHUNK, SSD_CHUNK
    x = xs.reshape(bsz, nc, l_c, SSD_GROUPS, SSD_HPG, SSD_HEAD_DIM)
    dtc = dt.reshape(bsz, nc, l_c, SSD_GROUPS, SSD_HPG)
    bc = bm.reshape(bsz, nc, l_c, SSD_GROUPS, SSD_STATE)
    cc = cm.reshape(bsz, nc, l_c, SSD_GROUPS, SSD_STATE)
    da = jnp.moveaxis(dtc * a.reshape(SSD_GROUPS, SSD_HPG), 2, -1)
    a_cs = jnp.cumsum(da, axis=-1)
    xdt = x * dtc[..., None]
    causal = jnp.tril(jnp.ones((l_c, l_c), dtype=bool))
    decay = jnp.exp(jnp.where(causal, a_cs[..., :, None] - a_cs[..., None, :], -jnp.inf))
    cb = jnp.einsum('bclgn,bcsgn->bcgls', cc, bc)
    y_diag = jnp.einsum('bcgrls,bcsgrp->bclgrp', cb[:, :, :, None] * decay, xdt)
    decay_states = jnp.exp(a_cs[..., -1:] - a_cs)
    states = jnp.einsum('bclgn,bcgrl,bclgrp->bcgrpn', bc, decay_states, xdt).astype(jnp.float32)
    chunk_decay = jnp.exp(a_cs[..., -1])

    def step(h, inp):
        st, dec = inp
        return h * dec[..., None, None] + st, h

    h0 = jnp.zeros((bsz, SSD_GROUPS, SSD_HPG, SSD_HEAD_DIM, SSD_STATE), jnp.float32)
    _, prev = lax.scan(step, h0, (jnp.moveaxis(states, 1, 0), jnp.moveaxis(chunk_decay, 1, 0)))
    prev = jnp.moveaxis(prev, 0, 1)
    y_off = jnp.einsum('bclgn,bcgrpn,bcgrl->bclgrp', cc, prev, jnp.exp(a_cs))
    return (y_diag + y_off).reshape(bsz, s_len, SSD_HEADS, SSD_HEAD_DIM).astype(xs.dtype)


def mamba2_mixer(h, w_in, conv_w, conv_b, dt_bias, a_log, d_skip, norm_g, w_out):
    bsz, s_len, _ = h.shape
    proj = h @ w_in
    z = proj[..., :SSD_D_INNER]
    xbc = proj[..., SSD_D_INNER:SSD_D_INNER + SSD_CONV_DIM]
    dt_raw = proj[..., SSD_D_INNER + SSD_CONV_DIM:]
    xbc = jax.nn.silu(causal_dwconv(xbc, conv_w, conv_b))
    gn = SSD_GROUPS * SSD_STATE
    xs = xbc[..., :SSD_D_INNER].reshape(bsz, s_len, SSD_HEADS, SSD_HEAD_DIM)
    bm = xbc[..., SSD_D_INNER:SSD_D_INNER + gn].reshape(bsz, s_len, SSD_GROUPS, SSD_STATE)
    cm = xbc[..., SSD_D_INNER + gn:].reshape(bsz, s_len, SSD_GROUPS, SSD_STATE)
    dt = jax.nn.softplus((dt_raw + dt_bias).astype(jnp.float32))
    a = -jnp.exp(a_log.astype(jnp.float32))
    y = ssd_chunked(xs, dt, a, bm, cm) + xs * d_skip[:, None]
    y = y.reshape(bsz, s_len, SSD_D_INNER)
    yz = (y * jax.nn.silu(z)).astype(jnp.float32).reshape(bsz, s_len, SSD_GROUPS, SSD_D_INNER // SSD_GROUPS)
    yz = yz * lax.rsqrt(jnp.mean(yz * yz, axis=-1, keepdims=True) + EPS)
    y = (yz.reshape(bsz, s_len, SSD_D_INNER) * norm_g.astype(jnp.float32)).astype(h.dtype)
    return y @ w_out


def fox_attention(h, w_in, b_f, q_norm_g, k_norm_g, w_out):
    bsz, s_len, _ = h.shape
    proj = h @ w_in
    q = rmsnorm(proj[..., :FOX_D].reshape(bsz, s_len, FOX_HEADS, FOX_HEAD_DIM), q_norm_g)
    k = rmsnorm(proj[..., FOX_D:2 * FOX_D].reshape(bsz, s_len, FOX_HEADS, FOX_HEAD_DIM), k_norm_g)
    v = proj[..., 2 * FOX_D:3 * FOX_D].reshape(bsz, s_len, FOX_HEADS, FOX_HEAD_DIM)
    gate = proj[..., 3 * FOX_D:4 * FOX_D]
    log_f = jax.nn.log_sigmoid((proj[..., 4 * FOX_D:] + b_f).astype(jnp.float32))
    cum = jnp.transpose(jnp.cumsum(log_f, axis=1), (0, 2, 1))
    q, k, v = (jnp.transpose(t, (0, 2, 1, 3)) for t in (q, k, v))
    scale = FOX_HEAD_DIM ** -0.5
    outs = []
    for i in range(s_len // Q_BLOCK):
        qs, qe = i * Q_BLOCK, (i + 1) * Q_BLOCK
        sc = jnp.einsum('bhqd,bhkd->bhqk', q[:, :, qs:qe], k[:, :, :qe]).astype(jnp.float32) * scale
        sc = sc + cum[:, :, qs:qe, None] - cum[:, :, None, :qe]
        mask = (qs + jnp.arange(Q_BLOCK))[:, None] >= jnp.arange(qe)[None, :]
        p = jax.nn.softmax(jnp.where(mask, sc, -jnp.inf), axis=-1)
        outs.append(jnp.einsum('bhqk,bhkd->bhqd', p.astype(v.dtype), v[:, :, :qe]))
    o = jnp.transpose(jnp.concatenate(outs, axis=2), (0, 2, 1, 3)).reshape(bsz, s_len, FOX_D)
    return (o * jax.nn.sigmoid(gate)) @ w_out


def conv_glu_ffn(h, w_up, conv_w, conv_b, w_down):
    u = h @ w_up
    gate = causal_dwconv(u[..., :D_FF], conv_w, conv_b)
    return (jax.nn.silu(gate) * u[..., D_FF:]) @ w_down


def setup_inputs(seed: int = 0) -> dict:
    key = jax.random.key(seed)
    ks = jax.random.split(key, 24)
    f32 = jnp.float32

    def nrm(k, shape, scale):
        return jax.random.normal(k, shape, f32) * scale

    res_scale = (2 * DEPTH) ** -0.5
    dt_init = jnp.exp(jax.random.uniform(ks[4], (N_SSD, SSD_HEADS), f32, math.log(1e-3), math.log(1e-1)))
    return {
        'x': nrm(ks[0], (BATCH, SEQ, D_MODEL), 1.0),
        'mix_norm_g': 1.0 + nrm(ks[1], (DEPTH, D_MODEL), 0.02),
        'ffn_norm_g': 1.0 + nrm(ks[2], (DEPTH, D_MODEL), 0.02),
        'ssd_w_in': nrm(ks[3], (N_SSD, D_MODEL, SSD_IN_DIM), D_MODEL ** -0.5),
        'ssd_conv_w': nrm(ks[5], (N_SSD, SSD_CONV, SSD_CONV_DIM), SSD_CONV ** -0.5),
        'ssd_conv_b': nrm(ks[6], (N_SSD, SSD_CONV_DIM), 0.02),
        'ssd_dt_bias': dt_init + jnp.log(-jnp.expm1(-dt_init)),
        'ssd_a_log': jnp.log(jax.random.uniform(ks[7], (N_SSD, SSD_HEADS), f32, 1.0, 16.0)),
        'ssd_d': 1.0 + nrm(ks[8], (N_SSD, SSD_HEADS), 0.1),
        'ssd_norm_g': 1.0 + nrm(ks[9], (N_SSD, SSD_D_INNER), 0.02),
        'ssd_w_out': nrm(ks[10], (N_SSD, SSD_D_INNER, D_MODEL), SSD_D_INNER ** -0.5 * res_scale),
        'fox_w_in': nrm(ks[11], (N_FOX, D_MODEL, FOX_IN_DIM), D_MODEL ** -0.5),
        'fox_b_f': jax.random.uniform(ks[12], (N_FOX, FOX_HEADS), f32, 2.0, 6.0),
        'fox_q_norm_g': 1.0 + nrm(ks[13], (N_FOX, FOX_HEAD_DIM), 0.02),
        'fox_k_norm_g': 1.0 + nrm(ks[14], (N_FOX, FOX_HEAD_DIM), 0.02),
        'fox_w_out': nrm(ks[15], (N_FOX, FOX_D, D_MODEL), FOX_D ** -0.5 * res_scale),
        'ffn_w_up': nrm(ks[16], (DEPTH, D_MODEL, 2 * D_FF), D_MODEL ** -0.5),
        'ffn_conv_w': nrm(ks[17], (DEPTH, FFN_CONV, D_FF), FFN_CONV ** -0.5),
        'ffn_conv_b': nrm(ks[18], (DEPTH, D_FF), 0.02),
        'ffn_w_down': nrm(ks[19], (DEPTH, D_FF, D_MODEL), D_FF ** -0.5 * res_scale),
        'final_norm_g': 1.0 + nrm(ks[20], (D_MODEL,), 0.02),
    }


def reference(x, mix_norm_g, ffn_norm_g,
              ssd_w_in, ssd_conv_w, ssd_conv_b, ssd_dt_bias, ssd_a_log, ssd_d, ssd_norm_g, ssd_w_out,
              fox_w_in, fox_b_f, fox_q_norm_g, fox_k_norm_g, fox_w_out,
              ffn_w_up, ffn_conv_w, ffn_conv_b, ffn_w_down, final_norm_g):
    h = x
    for i in range(DEPTH):
        hn = rmsnorm(h, mix_norm_g[i])
        j = i // N_MIXERS
        if i % N_MIXERS == 0:
            h = h + mamba2_mixer(hn, ssd_w_in[j], ssd_conv_w[j], ssd_conv_b[j], ssd_dt_bias[j],
                                 ssd_a_log[j], ssd_d[j], ssd_norm_g[j], ssd_w_out[j])
        else:
            h = h + fox_attention(hn, fox_w_in[j], fox_b_f[j], fox_q_norm_g[j], fox_k_norm_g[j], fox_w_out[j])
        h = h + conv_glu_ffn(rmsnorm(h, ffn_norm_g[i]), ffn_w_up[i], ffn_conv_w[i], ffn_conv_b[i], ffn_w_down[i])
    return rmsnorm(h, final_norm_g)
```

```python
import functools

import numpy as np
import jax
import jax.numpy as jnp
from jax import lax
from jax.experimental import pallas as pl
from jax.experimental.pallas import tpu as pltpu

F32 = jnp.float32
BF16 = jnp.bfloat16

EPS = 1e-6
LANES = 128
SUBLANES = 8
HEAD_DIM = 64
SSD_CHUNK = 128
SSD_GROUPS = 4
SSD_STATE = 128
SSD_CONV = 4
FFN_CONV = 3
VMEM_LIMIT_BYTES = 56 * 1024 * 1024


def _cparams(*sem):
    return pltpu.CompilerParams(dimension_semantics=sem, vmem_limit_bytes=VMEM_LIMIT_BYTES)


def _sigmoid(x):
    return 1.0 / (1.0 + jnp.exp(-x))


def _softplus(x):
    return jnp.maximum(x, 0.0) + jnp.log1p(jnp.exp(-jnp.abs(x)))


def _split_bf16(v, parts):
    out = []
    for _ in range(parts - 1):
        p = v.astype(BF16)
        out.append(p)
        v = v - p.astype(F32)
    out.append(v.astype(BF16))
    return out


def _dot(a, b):
    return jnp.dot(a, b, preferred_element_type=F32)


def _dot_nt(a, b):
    return lax.dot_general(a, b, (((1,), (1,)), ((), ())), preferred_element_type=F32)


def _select_dot(sel, v, parts):
    acc = None
    for p in _split_bf16(v, parts):
        t = _dot(sel, p)
        acc = t if acc is None else acc + t
    return acc


def _dot_select(v, sel, parts):
    acc = None
    for p in _split_bf16(v, parts):
        t = _dot(p, sel)
        acc = t if acc is None else acc + t
    return acc


def _norm_matmul_kernel(*refs, n_chunks, chunk, has_aux):
    if has_aux:
        x_ref, g_ref, w_ref, wa_ref, o_ref, oa_ref, xn_ref = refs
    else:
        x_ref, g_ref, w_ref, o_ref, xn_ref = refs

    @pl.when(pl.program_id(1) == 0)
    def _():
        x = x_ref[...]
        ms = jnp.mean(x * x, axis=-1, keepdims=True)
        xn_ref[...] = (x * lax.rsqrt(ms + EPS) * g_ref[...]).astype(BF16)
        if has_aux:
            oa_ref[...] = _dot(xn_ref[...], wa_ref[...])

    xn = xn_ref[...]
    for c in range(n_chunks):
        sl = slice(c * chunk, (c + 1) * chunk)
        o_ref[:, sl] = _dot(xn, w_ref[:, sl]).astype(o_ref.dtype)


def _norm_matmul(x, g, w, w_aux=None, *, tm, tn, chunk, name):
    t, d = x.shape
    n = w.shape[1]
    assert t % tm == 0 and n % tn == 0 and tn % chunk == 0
    has_aux = w_aux is not None
    in_specs = [
        pl.BlockSpec((tm, d), lambda i, j: (i, 0)),
        pl.BlockSpec((1, d), lambda i, j: (0, 0)),
        pl.BlockSpec((d, tn), lambda i, j: (0, j)),
    ]
    args = [x, g.reshape(1, d), w]
    out_shape = [jax.ShapeDtypeStruct((t, n), BF16)]
    out_specs = [pl.BlockSpec((tm, tn), lambda i, j: (i, j))]
    if has_aux:
        in_specs.append(pl.BlockSpec((d, LANES), lambda i, j: (0, 0)))
        args.append(w_aux)
        out_shape.append(jax.ShapeDtypeStruct((t, LANES), F32))
        out_specs.append(pl.BlockSpec((tm, LANES), lambda i, j: (i, 0)))
    res = pl.pallas_call(
        functools.partial(_norm_matmul_kernel, n_chunks=tn // chunk, chunk=chunk, has_aux=has_aux),
        grid=(t // tm, n // tn),
        in_specs=in_specs,
        out_specs=out_specs,
        out_shape=out_shape,
        scratch_shapes=[pltpu.VMEM((tm, d), BF16)],
        compiler_params=_cparams("parallel", "arbitrary"),
        name=name,
    )(*args)
    return res if has_aux else res[0]


def _matmul_residual_kernel(y_ref, w_ref, h_ref, o_ref):
    o_ref[...] = h_ref[...] + _dot(y_ref[...], w_ref[...])


def _matmul_residual(y, w, h, *, tm, name):
    t, k = y.shape
    d = w.shape[1]
    assert t % tm == 0
    return pl.pallas_call(
        _matmul_residual_kernel,
        grid=(t // tm,),
        in_specs=[
            pl.BlockSpec((tm, k), lambda i: (i, 0)),
            pl.BlockSpec((k, d), lambda i: (0, 0)),
            pl.BlockSpec((tm, d), lambda i: (i, 0)),
        ],
        out_specs=pl.BlockSpec((tm, d), lambda i: (i, 0)),
        out_shape=jax.ShapeDtypeStruct((t, d), F32),
        input_output_aliases={2: 0},
        compiler_params=_cparams("parallel"),
        name=name,
    )(y, w, h)


def _ffn_down_kernel(*refs, tm, seq_len, final):
    if final:
        ug_ref, uu_ref, uh_ref, cw_ref, cb_ref, w_ref, h_ref, fg_ref, o_ref, buf_ref = refs
    else:
        ug_ref, uu_ref, uh_ref, cw_ref, cb_ref, w_ref, h_ref, o_ref, buf_ref = refs
    seq_start = (pl.program_id(0) * tm) % seq_len == 0
    buf_ref[0:SUBLANES, :] = jnp.where(seq_start, 0.0, uh_ref[...].astype(F32))
    buf_ref[SUBLANES:, :] = ug_ref[...].astype(F32)
    conv = cb_ref[...]
    for k in range(FFN_CONV):
        off = SUBLANES - (FFN_CONV - 1) + k
        conv = conv + buf_ref[off:off + tm, :] * cw_ref[k:k + 1, :]
    act = conv * _sigmoid(conv) * uu_ref[...].astype(F32)
    out = h_ref[...] + _dot(act.astype(BF16), w_ref[...])
    if final:
        ms = jnp.mean(out * out, axis=-1, keepdims=True)
        out = out * lax.rsqrt(ms + EPS) * fg_ref[...]
    o_ref[...] = out


def _ffn_down(u, conv_w, conv_b, w, h, final_g, *, tm, seq_len, name):
    t, two_f = u.shape
    f = two_f // 2
    d = w.shape[1]
    assert t % tm == 0 and seq_len % tm == 0 and f % LANES == 0
    final = final_g is not None
    in_specs = [
        pl.BlockSpec((tm, f), lambda i: (i, 0)),
        pl.BlockSpec((tm, f), lambda i: (i, 1)),
        pl.BlockSpec((SUBLANES, f), lambda i: (jnp.maximum(i * (tm // SUBLANES) - 1, 0), 0)),
        pl.BlockSpec((FFN_CONV, f), lambda i: (0, 0)),
        pl.BlockSpec((1, f), lambda i: (0, 0)),
        pl.BlockSpec((f, d), lambda i: (0, 0)),
        pl.BlockSpec((tm, d), lambda i: (i, 0)),
    ]
    args = [u, u, u, conv_w, conv_b.reshape(1, f), w, h]
    if final:
        in_specs.append(pl.BlockSpec((1, d), lambda i: (0, 0)))
        args.append(final_g.reshape(1, d))
    return pl.pallas_call(
        functools.partial(_ffn_down_kernel, tm=tm, seq_len=seq_len, final=final),
        grid=(t // tm,),
        in_specs=in_specs,
        out_specs=pl.BlockSpec((tm, d), lambda i: (i, 0)),
        out_shape=jax.ShapeDtypeStruct((t, d), F32),
        scratch_shapes=[pltpu.VMEM((tm + SUBLANES, f), F32)],
        input_output_aliases={6: 0},
        compiler_params=_cparams("parallel"),
        name=name,
    )(*args)


def _ssd_kernel(z_ref, x_ref, bc_ref, xh_ref, bch_ref, dt_ref,
                cwx_ref, cbx_ref, cwbc_ref, cbbc_ref, dtb_ref, alog_ref, dsk_ref, ng_ref,
                e_ref, tri_ref, o_ref, state_ref, xbuf_ref, bcbuf_ref, *, ts):
    d_inner = x_ref.shape[1]
    gw = d_inner // SSD_GROUPS
    gn = SSD_GROUPS * SSD_STATE
    seq_start = pl.program_id(1) == 0

    @pl.when(seq_start)
    def _():
        state_ref[...] = jnp.zeros_like(state_ref)

    def conv_silu(src_ref, halo_ref, buf_ref, cw_ref, cb_ref):
        buf_ref[0:SUBLANES, :] = jnp.where(seq_start, 0.0, halo_ref[...].astype(F32))
        buf_ref[SUBLANES:, :] = src_ref[...].astype(F32)
        acc = cb_ref[...]
        for k in range(SSD_CONV):
            off = SUBLANES - (SSD_CONV - 1) + k
            acc = acc + buf_ref[off:off + ts, :] * cw_ref[k:k + 1, :]
        return acc * _sigmoid(acc)

    xs_all = conv_silu(x_ref, xh_ref, xbuf_ref, cwx_ref, cbx_ref)
    bc_all = conv_silu(bc_ref, bch_ref, bcbuf_ref, cwbc_ref, cbbc_ref)
    dt_all = _softplus(dt_ref[...] + dtb_ref[...])
    da_all = dt_all * (-jnp.exp(alog_ref[...]))

    ll = SSD_CHUNK
    row = lax.broadcasted_iota(jnp.int32, (ll, ll), 0)
    col = lax.broadcasted_iota(jnp.int32, (ll, ll), 1)
    causal = row >= col
    first_head = lax.broadcasted_iota(jnp.int32, (ll, LANES), 1) < HEAD_DIM
    expand = e_ref[...]

    for c in range(ts // ll):
        r0 = c * ll
        xs = xs_all[r0:r0 + ll]
        bc = bc_all[r0:r0 + ll]
        dt = dt_all[r0:r0 + ll]
        acs = _select_dot(tri_ref[...], da_all[r0:r0 + ll], 3)
        acs_t = acs.T
        tot = acs[ll - 1:ll, :]
        per_head = jnp.concatenate([jnp.exp(acs), dt, jnp.exp(tot - acs)], axis=0)
        per_chan = _dot_select(per_head, expand, 2)
        ea_x = per_chan[0:ll]
        dt_x = per_chan[ll:2 * ll]
        ds_x = per_chan[2 * ll:3 * ll]
        xdt = xs * dt_x
        x_state = (xdt * ds_x).astype(BF16)

        ys = []
        for g in range(SSD_GROUPS):
            b_g = bc[:, g * SSD_STATE:(g + 1) * SSD_STATE]
            c_g = bc[:, gn + g * SSD_STATE:gn + (g + 1) * SSD_STATE].astype(BF16)
            cb = _dot_nt(c_g, b_g.astype(BF16))
            st = state_ref[g]
            y_off = _dot(c_g, st.astype(BF16)) * ea_x[:, g * gw:(g + 1) * gw]
            y_diag = []
            for pr in range(gw // LANES):
                h0 = (g * gw + pr * LANES) // HEAD_DIM
                ms = []
                for hh in (h0, h0 + 1):
                    diff = acs[:, hh:hh + 1] - acs_t[hh:hh + 1, :]
                    dec = jnp.exp(jnp.where(causal, diff, -jnp.inf))
                    ms.append((cb * dec).astype(BF16))
                lhs = jnp.concatenate(ms, axis=1)
                xp = xdt[:, g * gw + pr * LANES:g * gw + (pr + 1) * LANES]
                rhs = jnp.concatenate([jnp.where(first_head, xp, 0.0),
                                       jnp.where(first_head, 0.0, xp)], axis=0).astype(BF16)
                y_diag.append(_dot(lhs, rhs))
            ys.append(jnp.concatenate(y_diag, axis=1) + y_off)
            new = _dot(b_g.T.astype(BF16), x_state[:, g * gw:(g + 1) * gw])
            state_ref[g] = st * ea_x[ll - 1:ll, g * gw:(g + 1) * gw] + new

        y = jnp.concatenate(ys, axis=1) + xs * dsk_ref[...]
        z = z_ref[r0:r0 + ll, :].astype(F32)
        yz = y * (z * _sigmoid(z))
        outs = []
        for g in range(SSD_GROUPS):
            blk = yz[:, g * gw:(g + 1) * gw]
            ms = jnp.mean(blk * blk, axis=-1, keepdims=True)
            outs.append(blk * lax.rsqrt(ms + EPS))
        o_ref[r0:r0 + ll, :] = (jnp.concatenate(outs, axis=1) * ng_ref[...]).astype(o_ref.dtype)


def _ssd_core(proj, dt_raw, conv_w, conv_b, dt_bias, a_log, d_skip, norm_g, *, batch, seq_len, ts, name):
    t = proj.shape[0]
    n_heads = d_skip.shape[0]
    d_inner = n_heads * HEAD_DIM
    gn = SSD_GROUPS * SSD_STATE
    assert proj.shape[1] == 2 * d_inner + 2 * gn and (2 * d_inner) % (2 * gn) == 0
    assert seq_len % ts == 0 and ts % SSD_CHUNK == 0 and n_heads <= LANES
    ns = seq_len // ts
    hb = ts // SUBLANES
    bc_blk = 2 * d_inner // (2 * gn)

    def pad_lanes(v):
        return jnp.pad(v.astype(F32), (0, LANES - v.shape[0])).reshape(1, LANES)

    expand = np.zeros((LANES, d_inner), np.float32)
    expand[np.arange(d_inner) // HEAD_DIM, np.arange(d_inner)] = 1.0
    tri = np.tril(np.ones((SSD_CHUNK, SSD_CHUNK), np.float32))

    def rows(b, s):
        return b * ns + s

    def halo(b, s):
        return jnp.maximum(rows(b, s) * hb - 1, 0)

    const = lambda b, s: (0, 0)
    return pl.pallas_call(
        functools.partial(_ssd_kernel, ts=ts),
        grid=(batch, ns),
        in_specs=[
            pl.BlockSpec((ts, d_inner), lambda b, s: (rows(b, s), 0)),
            pl.BlockSpec((ts, d_inner), lambda b, s: (rows(b, s), 1)),
            pl.BlockSpec((ts, 2 * gn), lambda b, s: (rows(b, s), bc_blk)),
            pl.BlockSpec((SUBLANES, d_inner), lambda b, s: (halo(b, s), 1)),
            pl.BlockSpec((SUBLANES, 2 * gn), lambda b, s: (halo(b, s), bc_blk)),
            pl.BlockSpec((ts, LANES), lambda b, s: (rows(b, s), 0)),
            pl.BlockSpec((SSD_CONV, d_inner), const),
            pl.BlockSpec((1, d_inner), const),
            pl.BlockSpec((SSD_CONV, 2 * gn), const),
            pl.BlockSpec((1, 2 * gn), const),
            pl.BlockSpec((1, LANES), const),
            pl.BlockSpec((1, LANES), const),
            pl.BlockSpec((1, d_inner), const),
            pl.BlockSpec((1, d_inner), const),
            pl.BlockSpec((LANES, d_inner), const),
            pl.BlockSpec((SSD_CHUNK, SSD_CHUNK), const),
        ],
        out_specs=pl.BlockSpec((ts, d_inner), lambda b, s: (rows(b, s), 0)),
        out_shape=jax.ShapeDtypeStruct((t, d_inner), BF16),
        scratch_shapes=[
            pltpu.VMEM((SSD_GROUPS, SSD_STATE, d_inner // SSD_GROUPS), F32),
            pltpu.VMEM((ts + SUBLANES, d_inner), F32),
            pltpu.VMEM((ts + SUBLANES, 2 * gn), F32),
        ],
        compiler_params=_cparams("parallel", "arbitrary"),
        name=name,
    )(proj, proj, proj, proj, proj, dt_raw,
      conv_w[:, :d_inner], conv_b[:d_inner].reshape(1, d_inner),
      conv_w[:, d_inner:], conv_b[d_inner:].reshape(1, 2 * gn),
      pad_lanes(dt_bias), pad_lanes(a_log),
      jnp.repeat(d_skip, HEAD_DIM).reshape(1, d_inner), norm_g.reshape(1, d_inner),
      jnp.asarray(expand, BF16), jnp.asarray(tri, BF16))


GATE_PARTS = 3


def _fox_prep_kernel(qk_ref, fl_ref, bf_ref, qg_ref, kg_ref, hs_ref, tri_ref, pq_ref, pk_ref,
                     oq_ref, ok_ref, carry_ref, *, ts, d, scale):
    @pl.when(pl.program_id(1) == 0)
    def _():
        carry_ref[...] = jnp.zeros_like(carry_ref)

    logit = fl_ref[...] + bf_ref[...]
    log_f = jnp.minimum(logit, 0.0) - jnp.log1p(jnp.exp(-jnp.abs(logit)))
    cum = _select_dot(tri_ref[...], log_f, 3) + carry_ref[...]
    carry_ref[...] = cum[ts - 1:ts, :]
    pieces = _split_bf16(cum, GATE_PARTS) + [jnp.ones((ts, LANES), BF16)]
    feats = jnp.concatenate(pieces, axis=1)
    gq = _dot(feats, pq_ref[...]).astype(BF16)
    gk = _dot(feats, pk_ref[...]).astype(BF16)

    n_blk = d // LANES
    for name, o_ref, g_ref, gate, mult in (("q", oq_ref, qg_ref, gq, scale), ("k", ok_ref, kg_ref, gk, 1.0)):
        base = 0 if name == "q" else d
        for j in range(n_blk):
            v = qk_ref[:, base + j * LANES:base + (j + 1) * LANES].astype(F32)
            ssq = _dot((v * v).astype(BF16), hs_ref[...])
            vn = v * lax.rsqrt(ssq * (1.0 / HEAD_DIM) + EPS) * (g_ref[:, j * LANES:(j + 1) * LANES] * mult)
            o_ref[:, 2 * j * LANES:(2 * j + 1) * LANES] = vn.astype(BF16)
            o_ref[:, (2 * j + 1) * LANES:(2 * j + 2) * LANES] = gate[:, j * LANES:(j + 1) * LANES]


def _gate_placements(d):
    n_pairs = d // LANES
    pq = np.zeros((4 * LANES, d), np.float32)
    pk = np.zeros((4 * LANES, d), np.float32)
    ones_row = GATE_PARTS * LANES
    for pair in range(n_pairs):
        for which in range(2):
            head = 2 * pair + which
            base = pair * LANES + which * 2 * GATE_PARTS
            for p in range(GATE_PARTS):
                pq[p * LANES + head, base + p] = 1.0
                pk[ones_row, base + p] = 1.0
                pq[ones_row, base + GATE_PARTS + p] = 1.0
                pk[p * LANES + head, base + GATE_PARTS + p] = -1.0
    return pq, pk


def _fox_prep(proj, flog, b_f, qg, kg, *, batch, seq_len, ts, name):
    t = proj.shape[0]
    n_heads = b_f.shape[0]
    d = n_heads * HEAD_DIM
    assert seq_len % ts == 0 and n_heads <= LANES
    ns = seq_len // ts
    pq, pk = _gate_placements(d)
    hs = np.kron(np.eye(LANES // HEAD_DIM, dtype=np.float32), np.ones((HEAD_DIM, HEAD_DIM), np.float32))
    tri = np.tril(np.ones((ts, ts), np.float32))
    const = lambda b, s: (0, 0)
    rows = lambda b, s: (b * ns + s, 0)
    return pl.pallas_call(
        functools.partial(_fox_prep_kernel, ts=ts, d=d, scale=HEAD_DIM ** -0.5),
        grid=(batch, ns),
        in_specs=[
            pl.BlockSpec((ts, 2 * d), rows),
            pl.BlockSpec((ts, LANES), rows),
            pl.BlockSpec((1, LANES), const),
            pl.BlockSpec((1, d), const),
            pl.BlockSpec((1, d), const),
            pl.BlockSpec((LANES, LANES), const),
            pl.BlockSpec((ts, ts), const),
            pl.BlockSpec((4 * LANES, d), const),
            pl.BlockSpec((4 * LANES, d), const),
        ],
        out_specs=[pl.BlockSpec((ts, 2 * d), rows), pl.BlockSpec((ts, 2 * d), rows)],
        out_shape=[jax.ShapeDtypeStruct((t, 2 * d), BF16), jax.ShapeDtypeStruct((t, 2 * d), BF16)],
        scratch_shapes=[pltpu.VMEM((1, LANES), F32)],
        compiler_params=_cparams("parallel", "arbitrary"),
        name=name,
    )(proj, flog,
      jnp.pad(b_f.astype(F32), (0, LANES - n_heads)).reshape(1, LANES),
      jnp.tile(qg, n_heads).reshape(1, d), jnp.tile(kg, n_heads).reshape(1, d),
      jnp.asarray(hs, BF16), jnp.asarray(tri, BF16), jnp.asarray(pq, BF16), jnp.asarray(pk, BF16))


def _fox_attn_kernel(q_ref, k_ref, v_ref, gate_ref, o_ref, m_ref, l_ref, acc_ref, *, tq):
    qi = pl.program_id(2)
    lane = lax.broadcasted_iota(jnp.int32, (1, 2 * LANES), 1)
    g0 = LANES
    head_lanes = (
        (lane < HEAD_DIM) | ((lane >= g0) & (lane < g0 + 2 * GATE_PARTS)),
        ((lane >= HEAD_DIM) & (lane < LANES)) | ((lane >= g0 + 2 * GATE_PARTS) & (lane < g0 + 4 * GATE_PARTS)),
    )
    q = q_ref[...]
    qs = [jnp.where(sel, q, jnp.zeros_like(q)) for sel in head_lanes]

    m_ref[...] = jnp.full_like(m_ref, -jnp.inf)
    l_ref[...] = jnp.zeros_like(l_ref)
    acc_ref[...] = jnp.zeros_like(acc_ref)

    def block(start, masked):
        kb = k_ref[pl.ds(start, tq), :]
        vb = v_ref[pl.ds(start, tq), :]
        for hd in range(2):
            s = _dot_nt(qs[hd], kb)
            if masked:
                row = lax.broadcasted_iota(jnp.int32, (tq, tq), 0)
                col = lax.broadcasted_iota(jnp.int32, (tq, tq), 1)
                s = jnp.where(row >= col, s, -jnp.inf)
            m_prev = m_ref[hd]
            m_new = jnp.maximum(m_prev, jnp.max(s, axis=1, keepdims=True))
            alpha = jnp.exp(m_prev - m_new)
            p = jnp.exp(s - m_new[:, 0:1])
            l_ref[hd] = alpha * l_ref[hd] + jnp.sum(p, axis=1, keepdims=True)
            acc_ref[hd] = alpha * acc_ref[hd] + _dot(p.astype(BF16), vb)
            m_ref[hd] = m_new

    def body(j, carry):
        block(pl.multiple_of(j * tq, tq), False)
        return carry

    lax.fori_loop(0, qi, body, 0)
    block(pl.multiple_of(qi * tq, tq), True)

    first_head = lax.broadcasted_iota(jnp.int32, (tq, LANES), 1) < HEAD_DIM
    o = jnp.where(first_head, acc_ref[0] / l_ref[0], acc_ref[1] / l_ref[1])
    o_ref[...] = (o * _sigmoid(gate_ref[...].astype(F32))).astype(o_ref.dtype)


def _fox_attn(qp, kp, proj, *, batch, seq_len, tq, name):
    t = qp.shape[0]
    d = qp.shape[1] // 2
    n_pairs = d // LANES
    assert seq_len % tq == 0
    nq = seq_len // tq
    v_blk = 2 * d // LANES
    g_blk = 3 * d // LANES
    return pl.pallas_call(
        functools.partial(_fox_attn_kernel, tq=tq),
        grid=(batch, n_pairs, nq),
        in_specs=[
            pl.BlockSpec((tq, 2 * LANES), lambda b, p, i: (b * nq + i, p)),
            pl.BlockSpec((seq_len, 2 * LANES), lambda b, p, i: (b, p)),
            pl.BlockSpec((seq_len, LANES), lambda b, p, i: (b, v_blk + p)),
            pl.BlockSpec((tq, LANES), lambda b, p, i: (b * nq + i, g_blk + p)),
        ],
        out_specs=pl.BlockSpec((tq, LANES), lambda b, p, i: (b * nq + i, p)),
        out_shape=jax.ShapeDtypeStruct((t, d), BF16),
        scratch_shapes=[
            pltpu.VMEM((2, tq, LANES), F32),
            pltpu.VMEM((2, tq, LANES), F32),
            pltpu.VMEM((2, tq, LANES), F32),
        ],
        compiler_params=_cparams("parallel", "parallel", "arbitrary"),
        name=name,
    )(qp, kp, proj, proj)


def _tile(n, pref):
    tile = min(n, pref)
    while n % tile:
        tile //= 2
    return tile


def kernel(x, mix_norm_g, ffn_norm_g, ssd_w_in, ssd_conv_w, ssd_conv_b, ssd_dt_bias, ssd_a_log, ssd_d,
           ssd_norm_g, ssd_w_out, fox_w_in, fox_b_f, fox_q_norm_g, fox_k_norm_g, fox_w_out,
           ffn_w_up, ffn_conv_w, ffn_conv_b, ffn_w_down, final_norm_g):
    batch, seq_len, d_model = x.shape
    t = batch * seq_len
    depth = mix_norm_g.shape[0]
    tm_proj = _tile(t, 1024)
    tm_out = _tile(seq_len, 512)
    ts_ssd = _tile(seq_len, 128)
    ts_prep = _tile(seq_len, 512)
    tq = _tile(seq_len, 512)

    def lane_pad(w):
        return jnp.pad(w, ((0, 0), (0, LANES - w.shape[1])))

    h = x.reshape(t, d_model)
    for i in range(depth):
        j = i // 2
        if i % 2 == 0:
            w_in = ssd_w_in[j]
            n_heads = ssd_d.shape[1]
            n_main = w_in.shape[1] - n_heads
            proj, dt_raw = _norm_matmul(
                h, mix_norm_g[i], w_in[:, :n_main].astype(BF16), lane_pad(w_in[:, n_main:]).astype(BF16),
                tm=tm_proj, tn=1024, chunk=512, name=f"ssd_in_proj_{j}")
            y = _ssd_core(proj, dt_raw, ssd_conv_w[j], ssd_conv_b[j], ssd_dt_bias[j], ssd_a_log[j],
                          ssd_d[j], ssd_norm_g[j], batch=batch, seq_len=seq_len, ts=ts_ssd,
                          name=f"ssd_core_{j}")
            h = _matmul_residual(y, ssd_w_out[j].astype(BF16), h, tm=tm_out, name=f"ssd_out_proj_{j}")
        else:
            w_in = fox_w_in[j]
            n_heads = fox_b_f.shape[1]
            n_main = w_in.shape[1] - n_heads
            proj, flog = _norm_matmul(
                h, mix_norm_g[i], w_in[:, :n_main].astype(BF16), lane_pad(w_in[:, n_main:]).astype(BF16),
                tm=tm_proj, tn=1024, chunk=512, name=f"fox_in_proj_{j}")
            qp, kp = _fox_prep(proj, flog, fox_b_f[j], fox_q_norm_g[j], fox_k_norm_g[j],
                               batch=batch, seq_len=seq_len, ts=ts_prep, name=f"fox_prep_{j}")
            o = _fox_attn(qp, kp, proj, batch=batch, seq_len=seq_len, tq=tq, name=f"fox_attn_{j}")
            h = _matmul_residual(o, fox_w_out[j].astype(BF16), h, tm=tm_out, name=f"fox_out_proj_{j}")
        u = _norm_matmul(h, ffn_norm_g[i], ffn_w_up[i].astype(BF16), tm=tm_proj, tn=512, chunk=512,
                         name=f"ffn_up_{i}")
        h = _ffn_down(u, ffn_conv_w[i], ffn_conv_b[i], ffn_w_down[i].astype(BF16), h,
                      final_norm_g if i == depth - 1 else None, tm=tm_out, seq_len=seq_len,
                      name=f"ffn_down_{i}")
    return h.reshape(batch, seq_len, d_model)
```

```python
import functools

import numpy as np
import jax
import jax.numpy as jnp
from jax import lax
from jax.experimental import pallas as pl
from jax.experimental.pallas import tpu as pltpu

F32 = jnp.float32
BF16 = jnp.bfloat16

EPS = 1e-6
LANES = 128
SUBLANES = 8
HEAD_DIM = 64
SSD_CHUNK = 128
SSD_GROUPS = 4
SSD_STATE = 128
SSD_CONV = 4
FFN_CONV = 3
VMEM_LIMIT_BYTES = 56 * 1024 * 1024


def _cparams(*sem):
    return pltpu.CompilerParams(dimension_semantics=sem, vmem_limit_bytes=VMEM_LIMIT_BYTES)


def _sigmoid(x):
    return 1.0 / (1.0 + jnp.exp(-x))


def _softplus(x):
    return jnp.maximum(x, 0.0) + jnp.log1p(jnp.exp(-jnp.abs(x)))


def _split_bf16(v, parts):
    out = []
    for _ in range(parts - 1):
        p = v.astype(BF16)
        out.append(p)
        v = v - p.astype(F32)
    out.append(v.astype(BF16))
    return out


def _dot(a, b):
    return jnp.dot(a, b, preferred_element_type=F32)


def _dot_nt(a, b):
    return lax.dot_general(a, b, (((1,), (1,)), ((), ())), preferred_element_type=F32)


def _select_dot(sel, v, parts):
    acc = None
    for p in _split_bf16(v, parts):
        t = _dot(sel, p)
        acc = t if acc is None else acc + t
    return acc


def _dot_select(v, sel, parts):
    acc = None
    for p in _split_bf16(v, parts):
        t = _dot(p, sel)
        acc = t if acc is None else acc + t
    return acc


def _norm_matmul_kernel(*refs, n_chunks, chunk, has_aux):
    if has_aux:
        x_ref, g_ref, w_ref, wa_ref, o_ref, oa_ref, xn_ref = refs
    else:
        x_ref, g_ref, w_ref, o_ref, xn_ref = refs

    @pl.when(pl.program_id(1) == 0)
    def _():
        x = x_ref[...]
        ms = jnp.mean(x * x, axis=-1, keepdims=True)
        xn_ref[...] = (x * lax.rsqrt(ms + EPS) * g_ref[...]).astype(BF16)
        if has_aux:
            oa_ref[...] = _dot(xn_ref[...], wa_ref[...])

    xn = xn_ref[...]
    for c in range(n_chunks):
        sl = slice(c * chunk, (c + 1) * chunk)
        o_ref[:, sl] = _dot(xn, w_ref[:, sl]).astype(o_ref.dtype)


def _norm_matmul(x, g, w, w_aux=None, *, tm, tn, chunk, name):
    t, d = x.shape
    n = w.shape[1]
    assert t % tm == 0 and n % tn == 0 and tn % chunk == 0
    has_aux = w_aux is not None
    in_specs = [
        pl.BlockSpec((tm, d), lambda i, j: (i, 0)),
        pl.BlockSpec((1, d), lambda i, j: (0, 0)),
        pl.BlockSpec((d, tn), lambda i, j: (0, j)),
    ]
    args = [x, g.reshape(1, d), w]
    out_shape = [jax.ShapeDtypeStruct((t, n), BF16)]
    out_specs = [pl.BlockSpec((tm, tn), lambda i, j: (i, j))]
    if has_aux:
        in_specs.append(pl.BlockSpec((d, LANES), lambda i, j: (0, 0)))
        args.append(w_aux)
        out_shape.append(jax.ShapeDtypeStruct((t, LANES), F32))
        out_specs.append(pl.BlockSpec((tm, LANES), lambda i, j: (i, 0)))
    res = pl.pallas_call(
        functools.partial(_norm_matmul_kernel, n_chunks=tn // chunk, chunk=chunk, has_aux=has_aux),
        grid=(t // tm, n // tn),
        in_specs=in_specs,
        out_specs=out_specs,
        out_shape=out_shape,
        scratch_shapes=[pltpu.VMEM((tm, d), BF16)],
        compiler_params=_cparams("parallel", "arbitrary"),
        name=name,
    )(*args)
    return res if has_aux else res[0]


def _matmul_residual_kernel(y_ref, w_ref, h_ref, o_ref):
    o_ref[...] = h_ref[...] + _dot(y_ref[...], w_ref[...])


def _matmul_residual(y, w, h, *, tm, name):
    t, k = y.shape
    d = w.shape[1]
    assert t % tm == 0
    return pl.pallas_call(
        _matmul_residual_kernel,
        grid=(t // tm,),
        in_specs=[
            pl.BlockSpec((tm, k), lambda i: (i, 0)),
            pl.BlockSpec((k, d), lambda i: (0, 0)),
            pl.BlockSpec((tm, d), lambda i: (i, 0)),
        ],
        out_specs=pl.BlockSpec((tm, d), lambda i: (i, 0)),
        out_shape=jax.ShapeDtypeStruct((t, d), F32),
        input_output_aliases={2: 0},
        compiler_params=_cparams("parallel"),
        name=name,
    )(y, w, h)


def _ffn_down_kernel(*refs, tm, seq_len, final):
    if final:
        ug_ref, uu_ref, uh_ref, cw_ref, cb_ref, w_ref, h_ref, fg_ref, o_ref, buf_ref = refs
    else:
        ug_ref, uu_ref, uh_ref, cw_ref, cb_ref, w_ref, h_ref, o_ref, buf_ref = refs
    seq_start = (pl.program_id(0) * tm) % seq_len == 0
    buf_ref[0:SUBLANES, :] = jnp.where(seq_start, 0.0, uh_ref[...].astype(F32))
    buf_ref[SUBLANES:, :] = ug_ref[...].astype(F32)
    conv = cb_ref[...]
    for k in range(FFN_CONV):
        off = SUBLANES - (FFN_CONV - 1) + k
        conv = conv + buf_ref[off:off + tm, :] * cw_ref[k:k + 1, :]
    act = conv * _sigmoid(conv) * uu_ref[...].astype(F32)
    out = h_ref[...] + _dot(act.astype(BF16), w_ref[...])
    if final:
        ms = jnp.mean(out * out, axis=-1, keepdims=True)
        out = out * lax.rsqrt(ms + EPS) * fg_ref[...]
    o_ref[...] = out


def _ffn_down(u, conv_w, conv_b, w, h, final_g, *, tm, seq_len, name):
    t, two_f = u.shape
    f = two_f // 2
    d = w.shape[1]
    assert t % tm == 0 and seq_len % tm == 0 and f % LANES == 0
    final = final_g is not None
    in_specs = [
        pl.BlockSpec((tm, f), lambda i: (i, 0)),
        pl.BlockSpec((tm, f), lambda i: (i, 1)),
        pl.BlockSpec((SUBLANES, f), lambda i: (jnp.maximum(i * (tm // SUBLANES) - 1, 0), 0)),
        pl.BlockSpec((FFN_CONV, f), lambda i: (0, 0)),
        pl.BlockSpec((1, f), lambda i: (0, 0)),
        pl.BlockSpec((f, d), lambda i: (0, 0)),
        pl.BlockSpec((tm, d), lambda i: (i, 0)),
    ]
    args = [u, u, u, conv_w, conv_b.reshape(1, f), w, h]
    if final:
        in_specs.append(pl.BlockSpec((1, d), lambda i: (0, 0)))
        args.append(final_g.reshape(1, d))
    return pl.pallas_call(
        functools.partial(_ffn_down_kernel, tm=tm, seq_len=seq_len, final=final),
        grid=(t // tm,),
        in_specs=in_specs,
        out_specs=pl.BlockSpec((tm, d), lambda i: (i, 0)),
        out_shape=jax.ShapeDtypeStruct((t, d), F32),
        scratch_shapes=[pltpu.VMEM((tm + SUBLANES, f), F32)],
        input_output_aliases={6: 0},
        compiler_params=_cparams("parallel"),
        name=name,
    )(*args)


def _ssd_kernel(z_ref, x_ref, bc_ref, xh_ref, bch_ref, dt_ref,
                cwx_ref, cbx_ref, cwbc_ref, cbbc_ref, dtb_ref, alog_ref, dsk_ref, ng_ref,
                e_ref, tri_ref, o_ref, state_ref, xbuf_ref, bcbuf_ref, *, ts):
    d_inner = x_ref.shape[1]
    gw = d_inner // SSD_GROUPS
    gn = SSD_GROUPS * SSD_STATE
    seq_start = pl.program_id(1) == 0

    @pl.when(seq_start)
    def _():
        state_ref[...] = jnp.zeros_like(state_ref)

    def conv_silu(src_ref, halo_ref, buf_ref, cw_ref, cb_ref):
        buf_ref[0:SUBLANES, :] = jnp.where(seq_start, 0.0, halo_ref[...].astype(F32))
        buf_ref[SUBLANES:, :] = src_ref[...].astype(F32)
        acc = cb_ref[...]
        for k in range(SSD_CONV):
            off = SUBLANES - (SSD_CONV - 1) + k
            acc = acc + buf_ref[off:off + ts, :] * cw_ref[k:k + 1, :]
        return acc * _sigmoid(acc)

    xs_all = conv_silu(x_ref, xh_ref, xbuf_ref, cwx_ref, cbx_ref)
    bc_all = conv_silu(bc_ref, bch_ref, bcbuf_ref, cwbc_ref, cbbc_ref)
    dt_all = _softplus(dt_ref[...] + dtb_ref[...])
    da_all = dt_all * (-jnp.exp(alog_ref[...]))

    ll = SSD_CHUNK
    row = lax.broadcasted_iota(jnp.int32, (ll, ll), 0)
    col = lax.broadcasted_iota(jnp.int32, (ll, ll), 1)
    causal = row >= col
    first_head = lax.broadcasted_iota(jnp.int32, (ll, LANES), 1) < HEAD_DIM
    expand = e_ref[...]

    for c in range(ts // ll):
        r0 = c * ll
        xs = xs_all[r0:r0 + ll]
        bc = bc_all[r0:r0 + ll]
        dt = dt_all[r0:r0 + ll]
        acs = _select_dot(tri_ref[...], da_all[r0:r0 + ll], 3)
        acs_t = acs.T
        tot = acs[ll - 1:ll, :]
        per_head = jnp.concatenate([jnp.exp(acs), dt, jnp.exp(tot - acs)], axis=0)
        per_chan = _dot_select(per_head, expand, 2)
        ea_x = per_chan[0:ll]
        dt_x = per_chan[ll:2 * ll]
        ds_x = per_chan[2 * ll:3 * ll]
        xdt = xs * dt_x
        x_state = (xdt * ds_x).astype(BF16)

        ys = []
        for g in range(SSD_GROUPS):
            b_g = bc[:, g * SSD_STATE:(g + 1) * SSD_STATE]
            c_g = bc[:, gn + g * SSD_STATE:gn + (g + 1) * SSD_STATE].astype(BF16)
            cb = _dot_nt(c_g, b_g.astype(BF16))
            st = state_ref[g]
            y_off = _dot(c_g, st.astype(BF16)) * ea_x[:, g * gw:(g + 1) * gw]
            y_diag = []
            for pr in range(gw // LANES):
                h0 = (g * gw + pr * LANES) // HEAD_DIM
                ms = []
                for hh in (h0, h0 + 1):
                    diff = acs[:, hh:hh + 1] - acs_t[hh:hh + 1, :]
                    dec = jnp.exp(jnp.where(causal, diff, -jnp.inf))
                    ms.append((cb * dec).astype(BF16))
                lhs = jnp.concatenate(ms, axis=1)
                xp = xdt[:, g * gw + pr * LANES:g * gw + (pr + 1) * LANES]
                rhs = jnp.concatenate([jnp.where(first_head, xp, 0.0),
                                       jnp.where(first_head, 0.0, xp)], axis=0).astype(BF16)
                y_diag.append(_dot(lhs, rhs))
            ys.append(jnp.concatenate(y_diag, axis=1) + y_off)
            new = _dot(b_g.T.astype(BF16), x_state[:, g * gw:(g + 1) * gw])
            state_ref[g] = st * ea_x[ll - 1:ll, g * gw:(g + 1) * gw] + new

        y = jnp.concatenate(ys, axis=1) + xs * dsk_ref[...]
        z = z_ref[r0:r0 + ll, :].astype(F32)
        yz = y * (z * _sigmoid(z))
        outs = []
        for g in range(SSD_GROUPS):
            blk = yz[:, g * gw:(g + 1) * gw]
            ms = jnp.mean(blk * blk, axis=-1, keepdims=True)
            outs.append(blk * lax.rsqrt(ms + EPS))
        o_ref[r0:r0 + ll, :] = (jnp.concatenate(outs, axis=1) * ng_ref[...]).astype(o_ref.dtype)


def _ssd_core(proj, dt_raw, conv_w, conv_b, dt_bias, a_log, d_skip, norm_g, *, batch, seq_len, ts, name):
    t = proj.shape[0]
    n_heads = d_skip.shape[0]
    d_inner = n_heads * HEAD_DIM
    gn = SSD_GROUPS * SSD_STATE
    assert proj.shape[1] == 2 * d_inner + 2 * gn and (2 * d_inner) % (2 * gn) == 0
    assert seq_len % ts == 0 and ts % SSD_CHUNK == 0 and n_heads <= LANES
    ns = seq_len // ts
    hb = ts // SUBLANES
    bc_blk = 2 * d_inner // (2 * gn)

    def pad_lanes(v):
        return jnp.pad(v.astype(F32), (0, LANES - v.shape[0])).reshape(1, LANES)

    expand = np.zeros((LANES, d_inner), np.float32)
    expand[np.arange(d_inner) // HEAD_DIM, np.arange(d_inner)] = 1.0
    tri = np.tril(np.ones((SSD_CHUNK, SSD_CHUNK), np.float32))

    def rows(b, s):
        return b * ns + s

    def halo(b, s):
        return jnp.maximum(rows(b, s) * hb - 1, 0)

    const = lambda b, s: (0, 0)
    return pl.pallas_call(
        functools.partial(_ssd_kernel, ts=ts),
        grid=(batch, ns),
        in_specs=[
            pl.BlockSpec((ts, d_inner), lambda b, s: (rows(b, s), 0)),
            pl.BlockSpec((ts, d_inner), lambda b, s: (rows(b, s), 1)),
            pl.BlockSpec((ts, 2 * gn), lambda b, s: (rows(b, s), bc_blk)),
            pl.BlockSpec((SUBLANES, d_inner), lambda b, s: (halo(b, s), 1)),
            pl.BlockSpec((SUBLANES, 2 * gn), lambda b, s: (halo(b, s), bc_blk)),
            pl.BlockSpec((ts, LANES), lambda b, s: (rows(b, s), 0)),
            pl.BlockSpec((SSD_CONV, d_inner), const),
            pl.BlockSpec((1, d_inner), const),
            pl.BlockSpec((SSD_CONV, 2 * gn), const),
            pl.BlockSpec((1, 2 * gn), const),
            pl.BlockSpec((1, LANES), const),
            pl.BlockSpec((1, LANES), const),
            pl.BlockSpec((1, d_inner), const),
            pl.BlockSpec((1, d_inner), const),
            pl.BlockSpec((LANES, d_inner), const),
            pl.BlockSpec((SSD_CHUNK, SSD_CHUNK), const),
        ],
        out_specs=pl.BlockSpec((ts, d_inner), lambda b, s: (rows(b, s), 0)),
        out_shape=jax.ShapeDtypeStruct((t, d_inner), BF16),
        scratch_shapes=[
            pltpu.VMEM((SSD_GROUPS, SSD_STATE, d_inner // SSD_GROUPS), F32),
            pltpu.VMEM((ts + SUBLANES, d_inner), F32),
            pltpu.VMEM((ts + SUBLANES, 2 * gn), F32),
        ],
        compiler_params=_cparams("parallel", "arbitrary"),
        name=name,
    )(proj, proj, proj, proj, proj, dt_raw,
      conv_w[:, :d_inner], conv_b[:d_inner].reshape(1, d_inner),
      conv_w[:, d_inner:], conv_b[d_inner:].reshape(1, 2 * gn),
      pad_lanes(dt_bias), pad_lanes(a_log),
      jnp.repeat(d_skip, HEAD_DIM).reshape(1, d_inner), norm_g.reshape(1, d_inner),
      jnp.asarray(expand, BF16), jnp.asarray(tri, BF16))


GATE_PARTS = 3
VT_ROWS = LANES + 16
LOG2_E = 1.4426950408889634


def _fox_prep_kernel(qk_ref, v_ref, fl_ref, bf_ref, qg_ref, kg_ref, hs_ref, tri_ref, pq_ref, pk_ref, eye_ref,
                     qt_ref, ok_ref, vt_ref, carry_ref, *, ts, tq, tk, d, scale):
    @pl.when(pl.program_id(1) == 0)
    def _():
        carry_ref[...] = jnp.zeros_like(carry_ref)

    for j in range(d // LANES):
        for kk in range(ts // tk):
            vj = v_ref[kk * tk:(kk + 1) * tk, j * LANES:(j + 1) * LANES]
            vt_ref[j, kk, 0:LANES, :] = _dot_nt(eye_ref[...], vj).astype(BF16)
            vt_ref[j, kk, LANES:, :] = jnp.ones((VT_ROWS - LANES, tk), BF16)

    logit = fl_ref[...] + bf_ref[...]
    log_f = jnp.minimum(logit, 0.0) - jnp.log1p(jnp.exp(-jnp.abs(logit)))
    cum = _select_dot(tri_ref[...], log_f, 3) + carry_ref[...]
    carry_ref[...] = cum[ts - 1:ts, :]
    pieces = _split_bf16(cum * LOG2_E, GATE_PARTS) + [jnp.ones((ts, LANES), BF16)]
    feats = jnp.concatenate(pieces, axis=1)
    gq = _dot(feats, pq_ref[...]).astype(BF16)
    gk = _dot(feats, pk_ref[...]).astype(BF16)

    def head_norm(base, j, g_ref, mult):
        v = qk_ref[:, base + j * LANES:base + (j + 1) * LANES].astype(F32)
        ssq = _dot((v * v).astype(BF16), hs_ref[...])
        vn = v * lax.rsqrt(ssq * (1.0 / HEAD_DIM) + EPS) * (g_ref[:, j * LANES:(j + 1) * LANES] * mult)
        return vn.astype(BF16)

    for j in range(d // LANES):
        ok_ref[:, 2 * j * LANES:(2 * j + 1) * LANES] = head_norm(d, j, kg_ref, 1.0)
        ok_ref[:, (2 * j + 1) * LANES:(2 * j + 2) * LANES] = gk[:, j * LANES:(j + 1) * LANES]
        qn_t = _dot_nt(eye_ref[...], head_norm(0, j, qg_ref, scale)).astype(BF16)
        gq_t = _dot_nt(eye_ref[...], gq[:, j * LANES:(j + 1) * LANES]).astype(BF16)
        for qq in range(ts // tq):
            qt_ref[j, qq, 0:LANES, :] = qn_t[:, qq * tq:(qq + 1) * tq]
            qt_ref[j, qq, LANES:, :] = gq_t[:, qq * tq:(qq + 1) * tq]


def _gate_placements(d):
    n_pairs = d // LANES
    pq = np.zeros((4 * LANES, d), np.float32)
    pk = np.zeros((4 * LANES, d), np.float32)
    ones_row = GATE_PARTS * LANES
    for pair in range(n_pairs):
        for which in range(2):
            head = 2 * pair + which
            base = pair * LANES + which * 2 * GATE_PARTS
            for p in range(GATE_PARTS):
                pq[p * LANES + head, base + p] = 1.0
                pk[ones_row, base + p] = 1.0
                pq[ones_row, base + GATE_PARTS + p] = 1.0
                pk[p * LANES + head, base + GATE_PARTS + p] = -1.0
    return pq, pk


def _fox_prep(proj, flog, b_f, qg, kg, *, batch, seq_len, ts, tq, tk, name):
    t = proj.shape[0]
    n_heads = b_f.shape[0]
    d = n_heads * HEAD_DIM
    assert seq_len % ts == 0 and ts % tk == 0 and ts % tq == 0 and n_heads <= LANES
    ns = seq_len // ts
    n_pairs = d // LANES
    pq, pk = _gate_placements(d)
    hs = np.kron(np.eye(LANES // HEAD_DIM, dtype=np.float32), np.ones((HEAD_DIM, HEAD_DIM), np.float32))
    tri = np.tril(np.ones((ts, ts), np.float32))
    const = lambda b, s: (0, 0)
    rows = lambda b, s: (b * ns + s, 0)
    return pl.pallas_call(
        functools.partial(_fox_prep_kernel, ts=ts, tq=tq, tk=tk, d=d, scale=HEAD_DIM ** -0.5 * LOG2_E),
        grid=(batch, ns),
        in_specs=[
            pl.BlockSpec((ts, 2 * d), rows),
            pl.BlockSpec((ts, d), lambda b, s: (b * ns + s, 2)),
            pl.BlockSpec((ts, LANES), rows),
            pl.BlockSpec((1, LANES), const),
            pl.BlockSpec((1, d), const),
            pl.BlockSpec((1, d), const),
            pl.BlockSpec((LANES, LANES), const),
            pl.BlockSpec((ts, ts), const),
            pl.BlockSpec((4 * LANES, d), const),
            pl.BlockSpec((4 * LANES, d), const),
            pl.BlockSpec((LANES, LANES), const),
        ],
        out_specs=[
            pl.BlockSpec((None, n_pairs, ts // tq, 2 * LANES, tq), lambda b, s: (b, 0, s, 0, 0)),
            pl.BlockSpec((ts, 2 * d), rows),
            pl.BlockSpec((None, n_pairs, ts // tk, VT_ROWS, tk), lambda b, s: (b, 0, s, 0, 0)),
        ],
        out_shape=[
            jax.ShapeDtypeStruct((batch, n_pairs, seq_len // tq, 2 * LANES, tq), BF16),
            jax.ShapeDtypeStruct((t, 2 * d), BF16),
            jax.ShapeDtypeStruct((batch, n_pairs, seq_len // tk, VT_ROWS, tk), BF16),
        ],
        scratch_shapes=[pltpu.VMEM((1, LANES), F32)],
        compiler_params=_cparams("parallel", "arbitrary"),
        name=name,
    )(proj, proj, flog,
      jnp.pad(b_f.astype(F32), (0, LANES - n_heads)).reshape(1, LANES),
      jnp.tile(qg, n_heads).reshape(1, d), jnp.tile(kg, n_heads).reshape(1, d),
      jnp.asarray(hs, BF16), jnp.asarray(tri, BF16), jnp.asarray(pq, BF16), jnp.asarray(pk, BF16),
      jnp.eye(LANES, dtype=BF16))


def _fox_attn_kernel(qt_ref, k_ref, vt_ref, gate_ref, o_ref, m_ref, acc_ref, s_ref, *, tq, tk):
    qi = pl.program_id(2)
    lane = lax.broadcasted_iota(jnp.int32, (2 * LANES, 1), 0)
    g0 = LANES
    head_lanes = (
        (lane < HEAD_DIM) | ((lane >= g0) & (lane < g0 + 2 * GATE_PARTS)),
        ((lane >= HEAD_DIM) & (lane < LANES)) | ((lane >= g0 + 2 * GATE_PARTS) & (lane < g0 + 4 * GATE_PARTS)),
    )
    qt = qt_ref[...]
    q2t = jnp.concatenate([jnp.where(sel, qt, jnp.zeros_like(qt)) for sel in head_lanes], axis=1)

    m_ref[...] = jnp.full_like(m_ref, -jnp.inf)
    acc_ref[...] = jnp.zeros_like(acc_ref)

    def scores(slot, j):
        kb = k_ref[pl.ds(pl.multiple_of(j * tk, tk), tk), :]
        s_ref[slot] = _dot(kb, q2t)

    def update(slot, j, diag):
        s = s_ref[slot]
        if diag is not None:
            key = lax.broadcasted_iota(jnp.int32, (tk, 2 * tq), 0) + diag
            qry = lax.broadcasted_iota(jnp.int32, (tk, 2 * tq), 1)
            s = jnp.where(jnp.where(qry >= tq, qry - tq, qry) >= key, s, -jnp.inf)
        m_prev = m_ref[slot]
        m_new = jnp.maximum(m_prev, jnp.max(s, axis=0, keepdims=True))
        m_safe = jnp.where(m_new == -jnp.inf, 0.0, m_new)
        p = jnp.exp2(s - m_safe[0:1, :])
        alpha = jnp.exp2(m_prev - m_safe)
        acc_ref[slot] = alpha[0:1, :] * acc_ref[slot] + _dot(vt_ref[j], p.astype(BF16))
        m_ref[slot] = m_new

    scores(0, 0)

    def pair(i, carry):
        scores(1, 2 * i + 1)
        update(0, 2 * i, None)
        scores(0, 2 * i + 2)
        update(1, 2 * i + 1, None)
        return carry

    lax.fori_loop(0, qi, pair, 0)
    scores(1, 2 * qi + 1)
    update(0, 2 * qi, 0)
    update(1, 2 * qi + 1, tk)

    m = jnp.maximum(m_ref[0], m_ref[1])
    w0 = jnp.exp2(m_ref[0] - m)
    w1 = jnp.exp2(m_ref[1] - m)
    acc = w0[0:1, :] * acc_ref[0] + w1[0:1, :] * acc_ref[1]
    o_t = jnp.concatenate([acc[0:HEAD_DIM, 0:tq] / acc[LANES:LANES + 1, 0:tq],
                           acc[HEAD_DIM:LANES, tq:] / acc[LANES:LANES + 1, tq:]], axis=0)
    o_ref[...] = (o_t.T * _sigmoid(gate_ref[...].astype(F32))).astype(o_ref.dtype)


def _fox_attn(qt, kp, vt, proj, *, batch, seq_len, tq, tk, name):
    t = kp.shape[0]
    d = kp.shape[1] // 2
    n_pairs = d // LANES
    nq = seq_len // tq
    nk = seq_len // tk
    assert seq_len % tq == 0 and tq == 2 * tk
    assert qt.shape == (batch, n_pairs, nq, 2 * LANES, tq) and vt.shape == (batch, n_pairs, nk, VT_ROWS, tk)
    g_blk = 3 * d // LANES
    return pl.pallas_call(
        functools.partial(_fox_attn_kernel, tq=tq, tk=tk),
        grid=(batch, n_pairs, nq),
        in_specs=[
            pl.BlockSpec((None, None, None, 2 * LANES, tq), lambda b, p, i: (b, p, i, 0, 0)),
            pl.BlockSpec((seq_len, 2 * LANES), lambda b, p, i: (b, p)),
            pl.BlockSpec((None, None, nk, VT_ROWS, tk), lambda b, p, i: (b, p, 0, 0, 0)),
            pl.BlockSpec((tq, LANES), lambda b, p, i: (b * nq + i, g_blk + p)),
        ],
        out_specs=pl.BlockSpec((tq, LANES), lambda b, p, i: (b * nq + i, p)),
        out_shape=jax.ShapeDtypeStruct((t, d), BF16),
        scratch_shapes=[
            pltpu.VMEM((2, SUBLANES, 2 * tq), F32),
            pltpu.VMEM((2, VT_ROWS, 2 * tq), F32),
            pltpu.VMEM((2, tk, 2 * tq), F32),
        ],
        compiler_params=_cparams("parallel", "parallel", "arbitrary"),
        name=name,
    )(qt, kp, vt, proj)


def _tile(n, pref):
    tile = min(n, pref)
    while n % tile:
        tile //= 2
    return tile


def kernel(x, mix_norm_g, ffn_norm_g, ssd_w_in, ssd_conv_w, ssd_conv_b, ssd_dt_bias, ssd_a_log, ssd_d,
           ssd_norm_g, ssd_w_out, fox_w_in, fox_b_f, fox_q_norm_g, fox_k_norm_g, fox_w_out,
           ffn_w_up, ffn_conv_w, ffn_conv_b, ffn_w_down, final_norm_g):
    batch, seq_len, d_model = x.shape
    t = batch * seq_len
    depth = mix_norm_g.shape[0]
    tm_proj = _tile(t, 1024)
    tm_out = _tile(seq_len, 512)
    ts_ssd = _tile(seq_len, 128)
    ts_prep = _tile(seq_len, 512)
    tq = _tile(seq_len, 512)

    def lane_pad(w):
        return jnp.pad(w, ((0, 0), (0, LANES - w.shape[1])))

    h = x.reshape(t, d_model)
    for i in range(depth):
        j = i // 2
        if i % 2 == 0:
            w_in = ssd_w_in[j]
            n_heads = ssd_d.shape[1]
            n_main = w_in.shape[1] - n_heads
            proj, dt_raw = _norm_matmul(
                h, mix_norm_g[i], w_in[:, :n_main].astype(BF16), lane_pad(w_in[:, n_main:]).astype(BF16),
                tm=tm_proj, tn=1024, chunk=512, name=f"ssd_in_proj_{j}")
            y = _ssd_core(proj, dt_raw, ssd_conv_w[j], ssd_conv_b[j], ssd_dt_bias[j], ssd_a_log[j],
                          ssd_d[j], ssd_norm_g[j], batch=batch, seq_len=seq_len, ts=ts_ssd,
                          name=f"ssd_core_{j}")
            h = _matmul_residual(y, ssd_w_out[j].astype(BF16), h, tm=tm_out, name=f"ssd_out_proj_{j}")
        else:
            w_in = fox_w_in[j]
            n_heads = fox_b_f.shape[1]
            n_main = w_in.shape[1] - n_heads
            proj, flog = _norm_matmul(
                h, mix_norm_g[i], w_in[:, :n_main].astype(BF16), lane_pad(w_in[:, n_main:]).astype(BF16),
                tm=tm_proj, tn=1024, chunk=512, name=f"fox_in_proj_{j}")
            qt, kp, vt = _fox_prep(proj, flog, fox_b_f[j], fox_q_norm_g[j], fox_k_norm_g[j],
                                   batch=batch, seq_len=seq_len, ts=ts_prep, tq=tq, tk=tq // 2,
                                   name=f"fox_prep_{j}")
            o = _fox_attn(qt, kp, vt, proj, batch=batch, seq_len=seq_len, tq=tq, tk=tq // 2,
                          name=f"fox_attn_{j}")
            h = _matmul_residual(o, fox_w_out[j].astype(BF16), h, tm=tm_out, name=f"fox_out_proj_{j}")
        u = _norm_matmul(h, ffn_norm_g[i], ffn_w_up[i].astype(BF16), tm=tm_proj, tn=512, chunk=512,
                         name=f"ffn_up_{i}")
        h = _ffn_down(u, ffn_conv_w[i], ffn_conv_b[i], ffn_w_down[i].astype(BF16), h,
                      final_norm_g if i == depth - 1 else None, tm=tm_out, seq_len=seq_len,
                      name=f"ffn_down_{i}")
    return h.reshape(batch, seq_len, d_model)
```

```python
import functools

import numpy as np
import jax
import jax.numpy as jnp
from jax import lax
from jax.experimental import pallas as pl
from jax.experimental.pallas import tpu as pltpu

F32 = jnp.float32
BF16 = jnp.bfloat16

EPS = 1e-6
LOG2_E = 1.4426950408889634
LANES = 128
SUBLANES = 8
HEAD_DIM = 64
SSD_CHUNK = 128
SSD_GROUPS = 4
SSD_STATE = 128
SSD_CONV = 4
FFN_CONV = 3
VMEM_LIMIT_BYTES = 56 * 1024 * 1024


def _cparams(*sem):
    return pltpu.CompilerParams(dimension_semantics=sem, vmem_limit_bytes=VMEM_LIMIT_BYTES)


def _sigmoid(x):
    return 1.0 / (1.0 + jnp.exp(-x))


def _softplus(x):
    return jnp.maximum(x, 0.0) + jnp.log1p(jnp.exp(-jnp.abs(x)))


def _split_bf16(v, parts):
    out = []
    for _ in range(parts - 1):
        p = v.astype(BF16)
        out.append(p)
        v = v - p.astype(F32)
    out.append(v.astype(BF16))
    return out


def _dot(a, b):
    return jnp.dot(a, b, preferred_element_type=F32)


def _dot_nt(a, b):
    return lax.dot_general(a, b, (((1,), (1,)), ((), ())), preferred_element_type=F32)


def _select_dot(sel, v, parts):
    acc = None
    for p in _split_bf16(v, parts):
        t = _dot(sel, p)
        acc = t if acc is None else acc + t
    return acc


def _rms_normed(x, g):
    ms = jnp.mean(x * x, axis=-1, keepdims=True)
    return (x * lax.rsqrt(ms + EPS) * g).astype(BF16)


def _resident(shape):
    return pl.BlockSpec(shape, lambda *_: (0,) * len(shape), pipeline_mode=pl.Buffered(1))


def _norm_matmul_kernel(x_ref, g_ref, w_ref, wa_ref, o_ref, oa_ref, *, chunk):
    xn = _rms_normed(x_ref[...], g_ref[...])
    oa_ref[...] = _dot(xn, wa_ref[...])
    for c in range(w_ref.shape[1] // chunk):
        sl = slice(c * chunk, (c + 1) * chunk)
        o_ref[:, sl] = _dot(xn, w_ref[:, sl]).astype(o_ref.dtype)


def _norm_matmul(x, g, w, w_aux, *, tm, chunk, name):
    t, d = x.shape
    n = w.shape[1]
    assert t % tm == 0 and n % chunk == 0
    return pl.pallas_call(
        functools.partial(_norm_matmul_kernel, chunk=chunk),
        grid=(t // tm,),
        in_specs=[
            pl.BlockSpec((tm, d), lambda i: (i, 0)),
            _resident((1, d)),
            _resident((d, n)),
            _resident((d, LANES)),
        ],
        out_specs=[pl.BlockSpec((tm, n), lambda i: (i, 0)), pl.BlockSpec((tm, LANES), lambda i: (i, 0))],
        out_shape=[jax.ShapeDtypeStruct((t, n), BF16), jax.ShapeDtypeStruct((t, LANES), F32)],
        compiler_params=_cparams("parallel"),
        name=name,
    )(x, g.reshape(1, d), w, w_aux)


def _matmul_residual_kernel(*refs, final):
    if final:
        y_ref, w_ref, h_ref, fg_ref, o_ref = refs
    else:
        y_ref, w_ref, h_ref, o_ref = refs
    out = h_ref[...] + _dot(y_ref[...], w_ref[...])
    if final:
        ms = jnp.mean(out * out, axis=-1, keepdims=True)
        out = out * lax.rsqrt(ms + EPS) * fg_ref[...]
    o_ref[...] = out


def _matmul_residual(y, w, h, final_g=None, *, tm, name, in_place=True):
    t, k = y.shape
    d = w.shape[1]
    assert t % tm == 0
    final = final_g is not None
    in_specs = [
        pl.BlockSpec((tm, k), lambda i: (i, 0)),
        _resident((k, d)),
        pl.BlockSpec((tm, d), lambda i: (i, 0)),
    ]
    args = [y, w, h]
    if final:
        in_specs.append(_resident((1, d)))
        args.append(final_g.reshape(1, d))
    return pl.pallas_call(
        functools.partial(_matmul_residual_kernel, final=final),
        grid=(t // tm,),
        in_specs=in_specs,
        out_specs=pl.BlockSpec((tm, d), lambda i: (i, 0)),
        out_shape=jax.ShapeDtypeStruct((t, d), F32),
        input_output_aliases={2: 0} if in_place else {},
        compiler_params=_cparams("parallel"),
        name=name,
    )(*args)


FFN_HALO = 16


def _ffn_up_kernel(x_ref, xh_ref, g_ref, w_ref, cw_ref, cb_ref, o_ref, xn_ref, buf_ref, *, tm, f, chunk, seq_len):
    seq_start = (pl.program_id(0) * tm) % seq_len == 0
    xn_ref[0:tm, :] = _rms_normed(x_ref[...], g_ref[...])
    xn_ref[tm:, :] = _rms_normed(xh_ref[...], g_ref[...])
    for c in range(f // chunk):
        sl = slice(c * chunk, (c + 1) * chunk)
        buf = buf_ref.at[c % 2]
        gate = _dot(xn_ref[...], w_ref[:, sl])
        up = _dot(xn_ref[0:tm, :], w_ref[:, f + c * chunk:f + (c + 1) * chunk])
        buf[0:SUBLANES, :] = jnp.where(seq_start, 0.0, gate[tm + FFN_HALO - SUBLANES:, :])
        buf[SUBLANES:, :] = gate[0:tm, :]
        conv = cb_ref[:, sl]
        for k in range(FFN_CONV):
            off = SUBLANES - (FFN_CONV - 1) + k
            conv = conv + buf[off:off + tm, :] * cw_ref[k:k + 1, sl]
        sig = 1.0 / (1.0 + jnp.exp2(conv * (-LOG2_E)))
        o_ref[:, sl] = (conv * sig * up).astype(o_ref.dtype)


def _ffn_up(h, g, w, conv_w, conv_b, *, tm, chunk, seq_len, name):
    t, d = h.shape
    f = w.shape[1] // 2
    assert t % tm == 0 and seq_len % tm == 0 and tm % FFN_HALO == 0 and f % chunk == 0
    return pl.pallas_call(
        functools.partial(_ffn_up_kernel, tm=tm, f=f, chunk=chunk, seq_len=seq_len),
        grid=(t // tm,),
        in_specs=[
            pl.BlockSpec((tm, d), lambda i: (i, 0)),
            pl.BlockSpec((FFN_HALO, d), lambda i: (jnp.maximum(i * (tm // FFN_HALO) - 1, 0), 0)),
            _resident((1, d)),
            _resident((d, 2 * f)),
            _resident((FFN_CONV, f)),
            _resident((1, f)),
        ],
        out_specs=pl.BlockSpec((tm, f), lambda i: (i, 0)),
        out_shape=jax.ShapeDtypeStruct((t, f), BF16),
        scratch_shapes=[
            pltpu.VMEM((tm + FFN_HALO, d), BF16),
            pltpu.VMEM((2, tm + SUBLANES, chunk), F32),
        ],
        compiler_params=_cparams("parallel"),
        name=name,
    )(h, h, g.reshape(1, d), w, conv_w, conv_b.reshape(1, f))


def _ssd_in_proj_kernel(x_ref, xh_ref, g_ref, w_ref, wa_ref, cw_ref, cb_ref, o_ref, oa_ref, xn_ref, buf_ref,
                        *, tm, d_inner, chunk, seq_len):
    seq_start = (pl.program_id(0) * tm) % seq_len == 0
    xn_ref[0:tm, :] = _rms_normed(x_ref[...], g_ref[...])
    xn_ref[tm:, :] = _rms_normed(xh_ref[...], g_ref[...])
    oa_ref[...] = _dot(xn_ref[0:tm, :], wa_ref[...])
    for c in range(w_ref.shape[1] // chunk):
        sl = slice(c * chunk, (c + 1) * chunk)
        if c * chunk < d_inner:
            pre = _dot(xn_ref[0:tm, :], w_ref[:, sl])
        else:
            csl = slice(c * chunk - d_inner, (c + 1) * chunk - d_inner)
            buf = buf_ref.at[c % 2]
            ext = _dot(xn_ref[...], w_ref[:, sl])
            buf[0:SUBLANES, :] = jnp.where(seq_start, 0.0, ext[tm + FFN_HALO - SUBLANES:, :])
            buf[SUBLANES:, :] = ext[0:tm, :]
            pre = cb_ref[:, csl]
            for k in range(SSD_CONV):
                off = SUBLANES - (SSD_CONV - 1) + k
                pre = pre + buf[off:off + tm, :] * cw_ref[k:k + 1, csl]
        o_ref[:, sl] = (pre / (1.0 + jnp.exp2(pre * (-LOG2_E)))).astype(o_ref.dtype)


def _ssd_in_proj(h, g, w, w_aux, conv_w, conv_b, *, d_inner, tm, chunk, seq_len, name):
    t, d = h.shape
    n = w.shape[1]
    n_conv = n - d_inner
    assert t % tm == 0 and seq_len % tm == 0 and tm % FFN_HALO == 0
    assert n % chunk == 0 and d_inner % chunk == 0 and conv_w.shape == (SSD_CONV, n_conv)
    return pl.pallas_call(
        functools.partial(_ssd_in_proj_kernel, tm=tm, d_inner=d_inner, chunk=chunk, seq_len=seq_len),
        grid=(t // tm,),
        in_specs=[
            pl.BlockSpec((tm, d), lambda i: (i, 0)),
            pl.BlockSpec((FFN_HALO, d), lambda i: (jnp.maximum(i * (tm // FFN_HALO) - 1, 0), 0)),
            _resident((1, d)),
            _resident((d, n)),
            _resident((d, LANES)),
            _resident((SSD_CONV, n_conv)),
            _resident((1, n_conv)),
        ],
        out_specs=[pl.BlockSpec((tm, n), lambda i: (i, 0)), pl.BlockSpec((tm, LANES), lambda i: (i, 0))],
        out_shape=[jax.ShapeDtypeStruct((t, n), BF16), jax.ShapeDtypeStruct((t, LANES), F32)],
        scratch_shapes=[
            pltpu.VMEM((tm + FFN_HALO, d), BF16),
            pltpu.VMEM((2, tm + SUBLANES, chunk), F32),
        ],
        compiler_params=_cparams("parallel"),
        name=name,
    )(h, h, g.reshape(1, d), w, w_aux, conv_w, conv_b.reshape(1, n_conv))


def _ssd_kernel(z_ref, x_ref, bc_ref, dt_ref, dtb_ref, alog_ref, dsk_ref, ng_ref,
                e_ref, tri_ref, o_ref, state_ref, *, ts):
    d_inner = x_ref.shape[1]
    gw = d_inner // SSD_GROUPS
    gn = SSD_GROUPS * SSD_STATE

    @pl.when(pl.program_id(1) == 0)
    def _():
        state_ref[...] = jnp.zeros_like(state_ref)

    dt_all = _softplus(dt_ref[...] + dtb_ref[...])
    da_all = dt_all * (-jnp.exp(alog_ref[...]))

    ll = SSD_CHUNK
    row = lax.broadcasted_iota(jnp.int32, (ll, ll), 0)
    col = lax.broadcasted_iota(jnp.int32, (ll, ll), 1)
    causal = row >= col
    first_head = lax.broadcasted_iota(jnp.int32, (ll, LANES), 1) < HEAD_DIM
    expand = e_ref[...]

    for c in range(ts // ll):
        r0 = c * ll
        xs = x_ref[r0:r0 + ll, :].astype(F32)
        bc = bc_ref[r0:r0 + ll, :]
        dt = dt_all[r0:r0 + ll]
        acs = _select_dot(tri_ref[...], da_all[r0:r0 + ll], 3)
        acs_t = acs.T
        tot = acs[ll - 1:ll, :]
        tile16 = (2 * SUBLANES, LANES)
        per_head = jnp.concatenate(
            [jnp.exp(acs).astype(BF16), dt.astype(BF16), jnp.exp(tot - acs).astype(BF16)]
            + [jnp.broadcast_to(p, tile16) for p in _split_bf16(jnp.exp(tot), 3)], axis=0)
        per_chan = _dot(per_head, expand)
        ea_x = per_chan[0:ll]
        dt_x = per_chan[ll:2 * ll]
        ds_x = per_chan[2 * ll:3 * ll]
        r1 = 3 * ll
        chunk_decay_x = (per_chan[r1:r1 + 1] + per_chan[r1 + 16:r1 + 17]) + per_chan[r1 + 32:r1 + 33]
        xdt = xs * dt_x
        x_state = (xdt * ds_x).astype(BF16)

        ys = []
        for g in range(SSD_GROUPS):
            b_g = bc[:, g * SSD_STATE:(g + 1) * SSD_STATE]
            c_g = bc[:, gn + g * SSD_STATE:gn + (g + 1) * SSD_STATE]
            cb = _dot_nt(c_g, b_g)
            st = state_ref[g]
            y_off = _dot(c_g, st.astype(BF16)) * ea_x[:, g * gw:(g + 1) * gw]
            y_diag = []
            for pr in range(gw // LANES):
                h0 = (g * gw + pr * LANES) // HEAD_DIM
                ms = []
                for hh in (h0, h0 + 1):
                    diff = acs[:, hh:hh + 1] - acs_t[hh:hh + 1, :]
                    dec = jnp.exp(jnp.where(causal, diff, -jnp.inf))
                    ms.append((cb * dec).astype(BF16))
                lhs = jnp.concatenate(ms, axis=1)
                xp = xdt[:, g * gw + pr * LANES:g * gw + (pr + 1) * LANES]
                rhs = jnp.concatenate([jnp.where(first_head, xp, 0.0),
                                       jnp.where(first_head, 0.0, xp)], axis=0).astype(BF16)
                y_diag.append(_dot(lhs, rhs))
            ys.append(jnp.concatenate(y_diag, axis=1) + y_off)
            new = _dot(b_g.astype(F32).T.astype(BF16), x_state[:, g * gw:(g + 1) * gw])
            state_ref[g] = st * chunk_decay_x[:, g * gw:(g + 1) * gw] + new

        y = jnp.concatenate(ys, axis=1) + xs * dsk_ref[...]
        yz = y * z_ref[r0:r0 + ll, :].astype(F32)
        outs = []
        for g in range(SSD_GROUPS):
            blk = yz[:, g * gw:(g + 1) * gw]
            ms = jnp.mean(blk * blk, axis=-1, keepdims=True)
            outs.append(blk * lax.rsqrt(ms + EPS))
        o_ref[r0:r0 + ll, :] = (jnp.concatenate(outs, axis=1) * ng_ref[...]).astype(o_ref.dtype)


def _ssd_core(proj, dt_raw, dt_bias, a_log, d_skip, norm_g, *, batch, seq_len, ts, name):
    t = proj.shape[0]
    n_heads = d_skip.shape[0]
    d_inner = n_heads * HEAD_DIM
    gn = SSD_GROUPS * SSD_STATE
    assert proj.shape[1] == 2 * d_inner + 2 * gn and (2 * d_inner) % (2 * gn) == 0
    assert seq_len % ts == 0 and ts % SSD_CHUNK == 0 and n_heads <= LANES
    ns = seq_len // ts
    bc_blk = 2 * d_inner // (2 * gn)

    def pad_lanes(v):
        return jnp.pad(v.astype(F32), (0, LANES - v.shape[0])).reshape(1, LANES)

    expand = np.zeros((LANES, d_inner), np.float32)
    expand[np.arange(d_inner) // HEAD_DIM, np.arange(d_inner)] = 1.0
    tri = np.tril(np.ones((SSD_CHUNK, SSD_CHUNK), np.float32))

    def rows(b, s):
        return b * ns + s

    return pl.pallas_call(
        functools.partial(_ssd_kernel, ts=ts),
        grid=(batch, ns),
        in_specs=[
            pl.BlockSpec((ts, d_inner), lambda b, s: (rows(b, s), 0)),
            pl.BlockSpec((ts, d_inner), lambda b, s: (rows(b, s), 1)),
            pl.BlockSpec((ts, 2 * gn), lambda b, s: (rows(b, s), bc_blk)),
            pl.BlockSpec((ts, LANES), lambda b, s: (rows(b, s), 0)),
            _resident((1, LANES)),
            _resident((1, LANES)),
            _resident((1, d_inner)),
            _resident((1, d_inner)),
            _resident((LANES, d_inner)),
            _resident((SSD_CHUNK, SSD_CHUNK)),
        ],
        out_specs=pl.BlockSpec((ts, d_inner), lambda b, s: (rows(b, s), 0)),
        out_shape=jax.ShapeDtypeStruct((t, d_inner), BF16),
        scratch_shapes=[pltpu.VMEM((SSD_GROUPS, SSD_STATE, d_inner // SSD_GROUPS), F32)],
        compiler_params=_cparams("parallel", "arbitrary"),
        name=name,
    )(proj, proj, proj, dt_raw,
      pad_lanes(dt_bias), pad_lanes(a_log),
      jnp.repeat(d_skip, HEAD_DIM).reshape(1, d_inner), norm_g.reshape(1, d_inner),
      jnp.asarray(expand, BF16), jnp.asarray(tri, BF16))


GATE_PARTS = 3
VT_ROWS = LANES + 16


def _fox_prep_kernel(qk_ref, v_ref, fl_ref, bf_ref, qg_ref, kg_ref, hs_ref, tri_ref, pq_ref, pk_ref, eye_ref,
                     qt_ref, ok_ref, vt_ref, carry_ref, *, ts, tq, tk, d, scale):
    @pl.when(pl.program_id(1) == 0)
    def _():
        carry_ref[...] = jnp.zeros_like(carry_ref)

    for j in range(d // LANES):
        for kk in range(ts // tk):
            vj = v_ref[kk * tk:(kk + 1) * tk, j * LANES:(j + 1) * LANES]
            vt_ref[j, kk, 0:LANES, :] = _dot_nt(eye_ref[...], vj).astype(BF16)
            vt_ref[j, kk, LANES:, :] = jnp.ones((VT_ROWS - LANES, tk), BF16)

    logit = fl_ref[...] + bf_ref[...]
    log_f = jnp.minimum(logit, 0.0) - jnp.log1p(jnp.exp(-jnp.abs(logit)))
    cum = _select_dot(tri_ref[...], log_f, 3) + carry_ref[...]
    carry_ref[...] = cum[ts - 1:ts, :]
    pieces = _split_bf16(cum * LOG2_E, GATE_PARTS) + [jnp.ones((ts, LANES), BF16)]
    feats = jnp.concatenate(pieces, axis=1)
    gq = _dot(feats, pq_ref[...]).astype(BF16)
    gk = _dot(feats, pk_ref[...]).astype(BF16)

    def head_norm(base, j, g_ref, mult):
        v = qk_ref[:, base + j * LANES:base + (j + 1) * LANES].astype(F32)
        ssq = _dot((v * v).astype(BF16), hs_ref[...])
        vn = v * lax.rsqrt(ssq * (1.0 / HEAD_DIM) + EPS) * (g_ref[:, j * LANES:(j + 1) * LANES] * mult)
        return vn.astype(BF16)

    for j in range(d // LANES):
        ok_ref[:, 2 * j * LANES:(2 * j + 1) * LANES] = head_norm(d, j, kg_ref, 1.0)
        ok_ref[:, (2 * j + 1) * LANES:(2 * j + 2) * LANES] = gk[:, j * LANES:(j + 1) * LANES]
        qn_t = _dot_nt(eye_ref[...], head_norm(0, j, qg_ref, scale)).astype(BF16)
        gq_t = _dot_nt(eye_ref[...], gq[:, j * LANES:(j + 1) * LANES]).astype(BF16)
        for qq in range(ts // tq):
            qt_ref[j, qq, 0:LANES, :] = qn_t[:, qq * tq:(qq + 1) * tq]
            qt_ref[j, qq, LANES:, :] = gq_t[:, qq * tq:(qq + 1) * tq]


def _gate_placements(d):
    n_pairs = d // LANES
    pq = np.zeros((4 * LANES, d), np.float32)
    pk = np.zeros((4 * LANES, d), np.float32)
    ones_row = GATE_PARTS * LANES
    for pair in range(n_pairs):
        for which in range(2):
            head = 2 * pair + which
            base = pair * LANES + which * 2 * GATE_PARTS
            for p in range(GATE_PARTS):
                pq[p * LANES + head, base + p] = 1.0
                pk[ones_row, base + p] = 1.0
                pq[ones_row, base + GATE_PARTS + p] = 1.0
                pk[p * LANES + head, base + GATE_PARTS + p] = -1.0
    return pq, pk


def _fox_prep(proj, flog, b_f, qg, kg, *, batch, seq_len, ts, tq, tk, name):
    t = proj.shape[0]
    n_heads = b_f.shape[0]
    d = n_heads * HEAD_DIM
    assert seq_len % ts == 0 and ts % tk == 0 and ts % tq == 0 and n_heads <= LANES
    ns = seq_len // ts
    n_pairs = d // LANES
    pq, pk = _gate_placements(d)
    hs = np.kron(np.eye(LANES // HEAD_DIM, dtype=np.float32), np.ones((HEAD_DIM, HEAD_DIM), np.float32))
    tri = np.tril(np.ones((ts, ts), np.float32))
    const = lambda b, s: (0, 0)
    rows = lambda b, s: (b * ns + s, 0)
    return pl.pallas_call(
        functools.partial(_fox_prep_kernel, ts=ts, tq=tq, tk=tk, d=d, scale=HEAD_DIM ** -0.5 * LOG2_E),
        grid=(batch, ns),
        in_specs=[
            pl.BlockSpec((ts, 2 * d), rows),
            pl.BlockSpec((ts, d), lambda b, s: (b * ns + s, 2)),
            pl.BlockSpec((ts, LANES), rows),
            pl.BlockSpec((1, LANES), const),
            pl.BlockSpec((1, d), const),
            pl.BlockSpec((1, d), const),
            pl.BlockSpec((LANES, LANES), const),
            pl.BlockSpec((ts, ts), const),
            pl.BlockSpec((4 * LANES, d), const),
            pl.BlockSpec((4 * LANES, d), const),
            pl.BlockSpec((LANES, LANES), const),
        ],
        out_specs=[
            pl.BlockSpec((None, n_pairs, ts // tq, 2 * LANES, tq), lambda b, s: (b, 0, s, 0, 0)),
            pl.BlockSpec((ts, 2 * d), rows),
            pl.BlockSpec((None, n_pairs, ts // tk, VT_ROWS, tk), lambda b, s: (b, 0, s, 0, 0)),
        ],
        out_shape=[
            jax.ShapeDtypeStruct((batch, n_pairs, seq_len // tq, 2 * LANES, tq), BF16),
            jax.ShapeDtypeStruct((t, 2 * d), BF16),
            jax.ShapeDtypeStruct((batch, n_pairs, seq_len // tk, VT_ROWS, tk), BF16),
        ],
        scratch_shapes=[pltpu.VMEM((1, LANES), F32)],
        compiler_params=_cparams("parallel", "arbitrary"),
        name=name,
    )(proj, proj, flog,
      jnp.pad(b_f.astype(F32), (0, LANES - n_heads)).reshape(1, LANES),
      jnp.tile(qg, n_heads).reshape(1, d), jnp.tile(kg, n_heads).reshape(1, d),
      jnp.asarray(hs, BF16), jnp.asarray(tri, BF16), jnp.asarray(pq, BF16), jnp.asarray(pk, BF16),
      jnp.eye(LANES, dtype=BF16))


def _fox_attn_kernel(qt_ref, k_ref, vt_ref, gate_ref, o_ref, m_ref, acc_ref, s_ref, *, tq, tk):
    qi = pl.program_id(2)
    lane = lax.broadcasted_iota(jnp.int32, (2 * LANES, 1), 0)
    g0 = LANES
    head_lanes = (
        (lane < HEAD_DIM) | ((lane >= g0) & (lane < g0 + 2 * GATE_PARTS)),
        ((lane >= HEAD_DIM) & (lane < LANES)) | ((lane >= g0 + 2 * GATE_PARTS) & (lane < g0 + 4 * GATE_PARTS)),
    )
    qt = qt_ref[...]
    q2t = jnp.concatenate([jnp.where(sel, qt, jnp.zeros_like(qt)) for sel in head_lanes], axis=1)

    m_ref[...] = jnp.full_like(m_ref, -jnp.inf)
    acc_ref[...] = jnp.zeros_like(acc_ref)

    def scores(slot, j):
        kb = k_ref[pl.ds(pl.multiple_of(j * tk, tk), tk), :]
        s_ref[slot] = _dot(kb, q2t)

    def update(slot, j, diag):
        s = s_ref[slot]
        if diag is not None:
            key = lax.broadcasted_iota(jnp.int32, (tk, 2 * tq), 0) + diag
            qry = lax.broadcasted_iota(jnp.int32, (tk, 2 * tq), 1)
            s = jnp.where(jnp.where(qry >= tq, qry - tq, qry) >= key, s, -jnp.inf)
        m_prev = m_ref[slot]
        m_new = jnp.maximum(m_prev, jnp.max(s, axis=0, keepdims=True))
        m_safe = jnp.where(m_new == -jnp.inf, 0.0, m_new)
        p = jnp.exp2(s - m_safe[0:1, :])
        alpha = jnp.exp2(m_prev - m_safe)
        acc_ref[slot] = alpha[0:1, :] * acc_ref[slot] + _dot(vt_ref[j], p.astype(BF16))
        m_ref[slot] = m_new

    scores(0, 0)

    def pair(i, carry):
        scores(1, 2 * i + 1)
        update(0, 2 * i, None)
        scores(0, 2 * i + 2)
        update(1, 2 * i + 1, None)
        return carry

    lax.fori_loop(0, qi, pair, 0)
    scores(1, 2 * qi + 1)
    update(0, 2 * qi, 0)
    update(1, 2 * qi + 1, tk)

    m = jnp.maximum(m_ref[0], m_ref[1])
    w0 = jnp.exp2(m_ref[0] - m)
    w1 = jnp.exp2(m_ref[1] - m)
    acc = w0[0:1, :] * acc_ref[0] + w1[0:1, :] * acc_ref[1]
    o_t = jnp.concatenate([acc[0:HEAD_DIM, 0:tq] / acc[LANES:LANES + 1, 0:tq],
                           acc[HEAD_DIM:LANES, tq:] / acc[LANES:LANES + 1, tq:]], axis=0)
    o_ref[...] = (o_t.T * _sigmoid(gate_ref[...].astype(F32))).astype(o_ref.dtype)


def _fox_attn(qt, kp, vt, proj, *, batch, seq_len, tq, tk, name):
    t = kp.shape[0]
    d = kp.shape[1] // 2
    n_pairs = d // LANES
    nq = seq_len // tq
    nk = seq_len // tk
    assert seq_len % tq == 0 and tq == 2 * tk
    assert qt.shape == (batch, n_pairs, nq, 2 * LANES, tq) and vt.shape == (batch, n_pairs, nk, VT_ROWS, tk)
    g_blk = 3 * d // LANES
    return pl.pallas_call(
        functools.partial(_fox_attn_kernel, tq=tq, tk=tk),
        grid=(batch, n_pairs, nq),
        in_specs=[
            pl.BlockSpec((None, None, None, 2 * LANES, tq), lambda b, p, i: (b, p, i, 0, 0)),
            pl.BlockSpec((seq_len, 2 * LANES), lambda b, p, i: (b, p)),
            pl.BlockSpec((None, None, nk, VT_ROWS, tk), lambda b, p, i: (b, p, 0, 0, 0)),
            pl.BlockSpec((tq, LANES), lambda b, p, i: (b * nq + i, g_blk + p)),
        ],
        out_specs=pl.BlockSpec((tq, LANES), lambda b, p, i: (b * nq + i, p)),
        out_shape=jax.ShapeDtypeStruct((t, d), BF16),
        scratch_shapes=[
            pltpu.VMEM((2, SUBLANES, 2 * tq), F32),
            pltpu.VMEM((2, VT_ROWS, 2 * tq), F32),
            pltpu.VMEM((2, tk, 2 * tq), F32),
        ],
        compiler_params=_cparams("parallel", "parallel", "arbitrary"),
        name=name,
    )(qt, kp, vt, proj)


def _tile(n, pref):
    tile = min(n, pref)
    while n % tile:
        tile //= 2
    return tile


def kernel(x, mix_norm_g, ffn_norm_g, ssd_w_in, ssd_conv_w, ssd_conv_b, ssd_dt_bias, ssd_a_log, ssd_d,
           ssd_norm_g, ssd_w_out, fox_w_in, fox_b_f, fox_q_norm_g, fox_k_norm_g, fox_w_out,
           ffn_w_up, ffn_conv_w, ffn_conv_b, ffn_w_down, final_norm_g):
    batch, seq_len, d_model = x.shape
    t = batch * seq_len
    depth = mix_norm_g.shape[0]
    tm_proj = _tile(seq_len, 512)
    tm_ffn = _tile(seq_len, 1024)
    tm_out = _tile(seq_len, 512)
    ts_ssd = _tile(seq_len, 256)
    ts_prep = _tile(seq_len, 512)
    tq = _tile(seq_len, 512)

    def lane_pad(w):
        return jnp.pad(w, ((0, 0), (0, LANES - w.shape[1])))

    h = x.reshape(t, d_model)
    for i in range(depth):
        j = i // 2
        if i % 2 == 0:
            w_in = ssd_w_in[j]
            n_heads = ssd_d.shape[1]
            n_main = w_in.shape[1] - n_heads
            proj, dt_raw = _ssd_in_proj(
                h, mix_norm_g[i], w_in[:, :n_main].astype(BF16), lane_pad(w_in[:, n_main:]).astype(BF16),
                ssd_conv_w[j], ssd_conv_b[j], d_inner=n_heads * HEAD_DIM,
                tm=tm_proj, chunk=512, seq_len=seq_len, name=f"ssd_in_proj_{j}")
            y = _ssd_core(proj, dt_raw, ssd_dt_bias[j], ssd_a_log[j], ssd_d[j], ssd_norm_g[j],
                          batch=batch, seq_len=seq_len, ts=ts_ssd, name=f"ssd_core_{j}")
            h = _matmul_residual(y, ssd_w_out[j].astype(BF16), h, tm=tm_out, name=f"ssd_out_proj_{j}",
                                 in_place=i > 0)
        else:
            w_in = fox_w_in[j]
            n_heads = fox_b_f.shape[1]
            n_main = w_in.shape[1] - n_heads
            proj, flog = _norm_matmul(
                h, mix_norm_g[i], w_in[:, :n_main].astype(BF16), lane_pad(w_in[:, n_main:]).astype(BF16),
                tm=tm_proj, chunk=512, name=f"fox_in_proj_{j}")
            qt, kp, vt = _fox_prep(proj, flog, fox_b_f[j], fox_q_norm_g[j], fox_k_norm_g[j],
                                   batch=batch, seq_len=seq_len, ts=ts_prep, tq=tq, tk=tq // 2,
                                   name=f"fox_prep_{j}")
            o = _fox_attn(qt, kp, vt, proj, batch=batch, seq_len=seq_len, tq=tq, tk=tq // 2,
                          name=f"fox_attn_{j}")
            h = _matmul_residual(o, fox_w_out[j].astype(BF16), h, tm=tm_out, name=f"fox_out_proj_{j}")
        act = _ffn_up(h, ffn_norm_g[i], ffn_w_up[i].astype(BF16), ffn_conv_w[i], ffn_conv_b[i],
                      tm=tm_ffn, chunk=256, seq_len=seq_len, name=f"ffn_up_{i}")
        h = _matmul_residual(act, ffn_w_down[i].astype(BF16), h, final_norm_g if i == depth - 1 else None,
                             tm=tm_out, name=f"ffn_down_{i}")
    return h.reshape(batch, seq_len, d_model)
```

```python
import functools

import numpy as np
import jax
import jax.numpy as jnp
from jax import lax
from jax.experimental import pallas as pl
from jax.experimental.pallas import tpu as pltpu

F32 = jnp.float32
BF16 = jnp.bfloat16

EPS = 1e-6
LOG2_E = 1.4426950408889634
LANES = 128
SUBLANES = 8
HEAD_DIM = 64
SSD_CHUNK = 128
SSD_GROUPS = 4
SSD_STATE = 128
SSD_CONV = 4
FFN_CONV = 3
VMEM_LIMIT_BYTES = 56 * 1024 * 1024


def _cparams(*sem):
    return pltpu.CompilerParams(dimension_semantics=sem, vmem_limit_bytes=VMEM_LIMIT_BYTES)


def _sigmoid(x):
    return 1.0 / (1.0 + jnp.exp(-x))


def _softplus(x):
    return jnp.maximum(x, 0.0) + jnp.log1p(jnp.exp(-jnp.abs(x)))


def _split_bf16(v, parts):
    out = []
    for _ in range(parts - 1):
        p = v.astype(BF16)
        out.append(p)
        v = v - p.astype(F32)
    out.append(v.astype(BF16))
    return out


def _dot(a, b):
    return jnp.dot(a, b, preferred_element_type=F32)


def _dot_nt(a, b):
    return lax.dot_general(a, b, (((1,), (1,)), ((), ())), preferred_element_type=F32)


def _select_dot(sel, v, parts):
    acc = None
    for p in _split_bf16(v, parts):
        t = _dot(sel, p)
        acc = t if acc is None else acc + t
    return acc


def _rms_normed(x, g):
    ms = jnp.mean(x * x, axis=-1, keepdims=True)
    return (x * lax.rsqrt(ms + EPS) * g).astype(BF16)


def _resident(shape):
    return pl.BlockSpec(shape, lambda *_: (0,) * len(shape), pipeline_mode=pl.Buffered(1))


def _norm_matmul_kernel(x_ref, g_ref, w_ref, wa_ref, o_ref, oa_ref, *, chunk):
    xn = _rms_normed(x_ref[...], g_ref[...])
    oa_ref[...] = _dot(xn, wa_ref[...])
    for c in range(w_ref.shape[1] // chunk):
        sl = slice(c * chunk, (c + 1) * chunk)
        o_ref[:, sl] = _dot(xn, w_ref[:, sl]).astype(o_ref.dtype)


def _norm_matmul(x, g, w, w_aux, *, tm, chunk, name):
    t, d = x.shape
    n = w.shape[1]
    assert t % tm == 0 and n % chunk == 0
    return pl.pallas_call(
        functools.partial(_norm_matmul_kernel, chunk=chunk),
        grid=(t // tm,),
        in_specs=[
            pl.BlockSpec((tm, d), lambda i: (i, 0)),
            _resident((1, d)),
            _resident((d, n)),
            _resident((d, LANES)),
        ],
        out_specs=[pl.BlockSpec((tm, n), lambda i: (i, 0)), pl.BlockSpec((tm, LANES), lambda i: (i, 0))],
        out_shape=[jax.ShapeDtypeStruct((t, n), BF16), jax.ShapeDtypeStruct((t, LANES), F32)],
        compiler_params=_cparams("parallel"),
        name=name,
    )(x, g.reshape(1, d), w, w_aux)


def _matmul_residual_kernel(*refs, final):
    if final:
        y_ref, w_ref, h_ref, fg_ref, o_ref = refs
    else:
        y_ref, w_ref, h_ref, o_ref = refs
    out = h_ref[...] + _dot(y_ref[...], w_ref[...])
    if final:
        ms = jnp.mean(out * out, axis=-1, keepdims=True)
        out = out * lax.rsqrt(ms + EPS) * fg_ref[...]
    o_ref[...] = out


def _matmul_residual(y, w, h, final_g=None, *, tm, name, in_place=True):
    t, k = y.shape
    d = w.shape[1]
    assert t % tm == 0
    final = final_g is not None
    in_specs = [
        pl.BlockSpec((tm, k), lambda i: (i, 0)),
        _resident((k, d)),
        pl.BlockSpec((tm, d), lambda i: (i, 0)),
    ]
    args = [y, w, h]
    if final:
        in_specs.append(_resident((1, d)))
        args.append(final_g.reshape(1, d))
    return pl.pallas_call(
        functools.partial(_matmul_residual_kernel, final=final),
        grid=(t // tm,),
        in_specs=in_specs,
        out_specs=pl.BlockSpec((tm, d), lambda i: (i, 0)),
        out_shape=jax.ShapeDtypeStruct((t, d), F32),
        input_output_aliases={2: 0} if in_place else {},
        compiler_params=_cparams("parallel"),
        name=name,
    )(*args)


FFN_HALO = 16


def _ffn_up_kernel(x_ref, xh_ref, g_ref, w_ref, cw_ref, cb_ref, o_ref, xn_ref, buf_ref, *, tm, f, chunk, seq_len):
    seq_start = (pl.program_id(0) * tm) % seq_len == 0
    xn_ref[0:tm, :] = _rms_normed(x_ref[...], g_ref[...])
    xn_ref[tm:, :] = _rms_normed(xh_ref[...], g_ref[...])
    for c in range(f // chunk):
        sl = slice(c * chunk, (c + 1) * chunk)
        buf = buf_ref.at[c % 2]
        gate = _dot(xn_ref[...], w_ref[:, sl])
        up = _dot(xn_ref[0:tm, :], w_ref[:, f + c * chunk:f + (c + 1) * chunk])
        buf[0:SUBLANES, :] = jnp.where(seq_start, 0.0, gate[tm + FFN_HALO - SUBLANES:, :])
        buf[SUBLANES:, :] = gate[0:tm, :]
        conv = cb_ref[:, sl]
        for k in range(FFN_CONV):
            off = SUBLANES - (FFN_CONV - 1) + k
            conv = conv + buf[off:off + tm, :] * cw_ref[k:k + 1, sl]
        sig = 1.0 / (1.0 + jnp.exp2(conv * (-LOG2_E)))
        o_ref[:, sl] = (conv * sig * up).astype(o_ref.dtype)


def _ffn_up(h, g, w, conv_w, conv_b, *, tm, chunk, seq_len, name):
    t, d = h.shape
    f = w.shape[1] // 2
    assert t % tm == 0 and seq_len % tm == 0 and tm % FFN_HALO == 0 and f % chunk == 0
    return pl.pallas_call(
        functools.partial(_ffn_up_kernel, tm=tm, f=f, chunk=chunk, seq_len=seq_len),
        grid=(t // tm,),
        in_specs=[
            pl.BlockSpec((tm, d), lambda i: (i, 0)),
            pl.BlockSpec((FFN_HALO, d), lambda i: (jnp.maximum(i * (tm // FFN_HALO) - 1, 0), 0)),
            _resident((1, d)),
            _resident((d, 2 * f)),
            _resident((FFN_CONV, f)),
            _resident((1, f)),
        ],
        out_specs=pl.BlockSpec((tm, f), lambda i: (i, 0)),
        out_shape=jax.ShapeDtypeStruct((t, f), BF16),
        scratch_shapes=[
            pltpu.VMEM((tm + FFN_HALO, d), BF16),
            pltpu.VMEM((2, tm + SUBLANES, chunk), F32),
        ],
        compiler_params=_cparams("parallel"),
        name=name,
    )(h, h, g.reshape(1, d), w, conv_w, conv_b.reshape(1, f))


def _ssd_in_proj_kernel(x_ref, xh_ref, g_ref, w_ref, wa_ref, cw_ref, cb_ref, o_ref, oa_ref, xn_ref, buf_ref,
                        *, tm, d_inner, chunk, seq_len):
    seq_start = (pl.program_id(0) * tm) % seq_len == 0
    xn_ref[0:tm, :] = _rms_normed(x_ref[...], g_ref[...])
    xn_ref[tm:, :] = _rms_normed(xh_ref[...], g_ref[...])
    oa_ref[...] = _dot(xn_ref[0:tm, :], wa_ref[...])
    for c in range(w_ref.shape[1] // chunk):
        sl = slice(c * chunk, (c + 1) * chunk)
        if c * chunk < d_inner:
            pre = _dot(xn_ref[0:tm, :], w_ref[:, sl])
        else:
            csl = slice(c * chunk - d_inner, (c + 1) * chunk - d_inner)
            buf = buf_ref.at[c % 2]
            ext = _dot(xn_ref[...], w_ref[:, sl])
            buf[0:SUBLANES, :] = jnp.where(seq_start, 0.0, ext[tm + FFN_HALO - SUBLANES:, :])
            buf[SUBLANES:, :] = ext[0:tm, :]
            pre = cb_ref[:, csl]
            for k in range(SSD_CONV):
                off = SUBLANES - (SSD_CONV - 1) + k
                pre = pre + buf[off:off + tm, :] * cw_ref[k:k + 1, csl]
        o_ref[:, sl] = (pre / (1.0 + jnp.exp2(pre * (-LOG2_E)))).astype(o_ref.dtype)


def _ssd_in_proj(h, g, w, w_aux, conv_w, conv_b, *, d_inner, tm, chunk, seq_len, name):
    t, d = h.shape
    n = w.shape[1]
    n_conv = n - d_inner
    assert t % tm == 0 and seq_len % tm == 0 and tm % FFN_HALO == 0
    assert n % chunk == 0 and d_inner % chunk == 0 and conv_w.shape == (SSD_CONV, n_conv)
    return pl.pallas_call(
        functools.partial(_ssd_in_proj_kernel, tm=tm, d_inner=d_inner, chunk=chunk, seq_len=seq_len),
        grid=(t // tm,),
        in_specs=[
            pl.BlockSpec((tm, d), lambda i: (i, 0)),
            pl.BlockSpec((FFN_HALO, d), lambda i: (jnp.maximum(i * (tm // FFN_HALO) - 1, 0), 0)),
            _resident((1, d)),
            _resident((d, n)),
            _resident((d, LANES)),
            _resident((SSD_CONV, n_conv)),
            _resident((1, n_conv)),
        ],
        out_specs=[pl.BlockSpec((tm, n), lambda i: (i, 0)), pl.BlockSpec((tm, LANES), lambda i: (i, 0))],
        out_shape=[jax.ShapeDtypeStruct((t, n), BF16), jax.ShapeDtypeStruct((t, LANES), F32)],
        scratch_shapes=[
            pltpu.VMEM((tm + FFN_HALO, d), BF16),
            pltpu.VMEM((2, tm + SUBLANES, chunk), F32),
        ],
        compiler_params=_cparams("parallel"),
        name=name,
    )(h, h, g.reshape(1, d), w, w_aux, conv_w, conv_b.reshape(1, n_conv))


def _ssd_kernel(z_ref, x_ref, bc_ref, dt_ref, dtb_ref, alog_ref, dsk_ref, ng_ref,
                e_ref, tri_ref, o_ref, state_ref, *, ts):
    d_inner = x_ref.shape[1]
    gw = d_inner // SSD_GROUPS
    gn = SSD_GROUPS * SSD_STATE

    @pl.when(pl.program_id(1) == 0)
    def _():
        state_ref[...] = jnp.zeros_like(state_ref)

    dt_all = _softplus(dt_ref[...] + dtb_ref[...])
    da_all = dt_all * (-jnp.exp(alog_ref[...]))

    ll = SSD_CHUNK
    row = lax.broadcasted_iota(jnp.int32, (ll, ll), 0)
    col = lax.broadcasted_iota(jnp.int32, (ll, ll), 1)
    causal = row >= col
    first_head = lax.broadcasted_iota(jnp.int32, (ll, LANES), 1) < HEAD_DIM
    expand = e_ref[...]

    for c in range(ts // ll):
        r0 = c * ll
        xs = x_ref[r0:r0 + ll, :].astype(F32)
        bc = bc_ref[r0:r0 + ll, :]
        dt = dt_all[r0:r0 + ll]
        acs = _select_dot(tri_ref[...], da_all[r0:r0 + ll], 3)
        acs_t = acs.T
        tot = acs[ll - 1:ll, :]
        tile16 = (2 * SUBLANES, LANES)
        per_head = jnp.concatenate(
            [jnp.exp(acs).astype(BF16), dt.astype(BF16), jnp.exp(tot - acs).astype(BF16)]
            + [jnp.broadcast_to(p, tile16) for p in _split_bf16(jnp.exp(tot), 3)], axis=0)
        per_chan = _dot(per_head, expand)
        ea_x = per_chan[0:ll]
        dt_x = per_chan[ll:2 * ll]
        ds_x = per_chan[2 * ll:3 * ll]
        r1 = 3 * ll
        chunk_decay_x = (per_chan[r1:r1 + 1] + per_chan[r1 + 16:r1 + 17]) + per_chan[r1 + 32:r1 + 33]
        xdt = xs * dt_x
        x_state = (xdt * ds_x).astype(BF16)

        ys = []
        for g in range(SSD_GROUPS):
            b_g = bc[:, g * SSD_STATE:(g + 1) * SSD_STATE]
            c_g = bc[:, gn + g * SSD_STATE:gn + (g + 1) * SSD_STATE]
            cb = _dot_nt(c_g, b_g)
            st = state_ref[g]
            y_off = _dot(c_g, st.astype(BF16)) * ea_x[:, g * gw:(g + 1) * gw]
            y_diag = []
            for pr in range(gw // LANES):
                h0 = (g * gw + pr * LANES) // HEAD_DIM
                ms = []
                for hh in (h0, h0 + 1):
                    diff = acs[:, hh:hh + 1] - acs_t[hh:hh + 1, :]
                    dec = jnp.exp(jnp.where(causal, diff, -jnp.inf))
                    ms.append((cb * dec).astype(BF16))
                lhs = jnp.concatenate(ms, axis=1)
                xp = xdt[:, g * gw + pr * LANES:g * gw + (pr + 1) * LANES]
                rhs = jnp.concatenate([jnp.where(first_head, xp, 0.0),
                                       jnp.where(first_head, 0.0, xp)], axis=0).astype(BF16)
                y_diag.append(_dot(lhs, rhs))
            ys.append(jnp.concatenate(y_diag, axis=1) + y_off)
            new = _dot(b_g.astype(F32).T.astype(BF16), x_state[:, g * gw:(g + 1) * gw])
            state_ref[g] = st * chunk_decay_x[:, g * gw:(g + 1) * gw] + new

        y = jnp.concatenate(ys, axis=1) + xs * dsk_ref[...]
        yz = y * z_ref[r0:r0 + ll, :].astype(F32)
        outs = []
        for g in range(SSD_GROUPS):
            blk = yz[:, g * gw:(g + 1) * gw]
            ms = jnp.mean(blk * blk, axis=-1, keepdims=True)
            outs.append(blk * lax.rsqrt(ms + EPS))
        o_ref[r0:r0 + ll, :] = (jnp.concatenate(outs, axis=1) * ng_ref[...]).astype(o_ref.dtype)


def _ssd_core(proj, dt_raw, dt_bias, a_log, d_skip, norm_g, *, batch, seq_len, ts, name):
    t = proj.shape[0]
    n_heads = d_skip.shape[0]
    d_inner = n_heads * HEAD_DIM
    gn = SSD_GROUPS * SSD_STATE
    assert proj.shape[1] == 2 * d_inner + 2 * gn and (2 * d_inner) % (2 * gn) == 0
    assert seq_len % ts == 0 and ts % SSD_CHUNK == 0 and n_heads <= LANES
    ns = seq_len // ts
    bc_blk = 2 * d_inner // (2 * gn)

    def pad_lanes(v):
        return jnp.pad(v.astype(F32), (0, LANES - v.shape[0])).reshape(1, LANES)

    expand = np.zeros((LANES, d_inner), np.float32)
    expand[np.arange(d_inner) // HEAD_DIM, np.arange(d_inner)] = 1.0
    tri = np.tril(np.ones((SSD_CHUNK, SSD_CHUNK), np.float32))

    def rows(b, s):
        return b * ns + s

    return pl.pallas_call(
        functools.partial(_ssd_kernel, ts=ts),
        grid=(batch, ns),
        in_specs=[
            pl.BlockSpec((ts, d_inner), lambda b, s: (rows(b, s), 0)),
            pl.BlockSpec((ts, d_inner), lambda b, s: (rows(b, s), 1)),
            pl.BlockSpec((ts, 2 * gn), lambda b, s: (rows(b, s), bc_blk)),
            pl.BlockSpec((ts, LANES), lambda b, s: (rows(b, s), 0)),
            _resident((1, LANES)),
            _resident((1, LANES)),
            _resident((1, d_inner)),
            _resident((1, d_inner)),
            _resident((LANES, d_inner)),
            _resident((SSD_CHUNK, SSD_CHUNK)),
        ],
        out_specs=pl.BlockSpec((ts, d_inner), lambda b, s: (rows(b, s), 0)),
        out_shape=jax.ShapeDtypeStruct((t, d_inner), BF16),
        scratch_shapes=[pltpu.VMEM((SSD_GROUPS, SSD_STATE, d_inner // SSD_GROUPS), F32)],
        compiler_params=_cparams("parallel", "arbitrary"),
        name=name,
    )(proj, proj, proj, dt_raw,
      pad_lanes(dt_bias), pad_lanes(a_log),
      jnp.repeat(d_skip, HEAD_DIM).reshape(1, d_inner), norm_g.reshape(1, d_inner),
      jnp.asarray(expand, BF16), jnp.asarray(tri, BF16))


GATE_PARTS = 3
VT_ROWS = LANES + 16
KEY_BLOCKS_PER_QUERY_BLOCK = 4
CUM_BLOCK = 256


def _fox_prep_kernel(qk_ref, v_ref, fl_ref, bf_ref, qg_ref, kg_ref, hs_ref, tri_ref, pq_ref, pk_ref, eye_ref,
                     qt_ref, ok_ref, vt_ref, carry_ref, *, ts, tq, tk, d, scale):
    @pl.when(pl.program_id(1) == 0)
    def _():
        carry_ref[...] = jnp.zeros_like(carry_ref)

    for j in range(d // LANES):
        for kk in range(ts // tk):
            vj = v_ref[kk * tk:(kk + 1) * tk, j * LANES:(j + 1) * LANES]
            vt_ref[j, kk, 0:LANES, :] = _dot_nt(eye_ref[...], vj).astype(BF16)
            vt_ref[j, kk, LANES:, :] = jnp.ones((VT_ROWS - LANES, tk), BF16)

    logit = fl_ref[...] + bf_ref[...]
    log_f = jnp.minimum(logit, 0.0) - jnp.log1p(jnp.exp(-jnp.abs(logit)))
    carry = carry_ref[...]
    cum_blocks = []
    for r in range(ts // CUM_BLOCK):
        c = _select_dot(tri_ref[...], log_f[r * CUM_BLOCK:(r + 1) * CUM_BLOCK], 3) + carry
        cum_blocks.append(c)
        carry = c[CUM_BLOCK - 1:CUM_BLOCK, :]
    carry_ref[...] = carry
    cum = jnp.concatenate(cum_blocks, axis=0)
    pieces = _split_bf16(cum * LOG2_E, GATE_PARTS) + [jnp.ones((ts, LANES), BF16)]
    feats = jnp.concatenate(pieces, axis=1)
    gq = _dot(feats, pq_ref[...]).astype(BF16)
    gk = _dot(feats, pk_ref[...]).astype(BF16)

    def head_norm(base, jj, g_ref, mult):
        sl = slice(2 * jj * LANES, (2 * jj + 2) * LANES)
        v = qk_ref[:, base + 2 * jj * LANES:base + (2 * jj + 2) * LANES].astype(F32)
        ssq = _dot((v * v).astype(BF16), hs_ref[...])
        vn = v * lax.rsqrt(ssq * (1.0 / HEAD_DIM) + EPS) * (g_ref[:, sl] * mult)
        return vn.astype(BF16)

    for jj in range(d // (2 * LANES)):
        kn = head_norm(d, jj, kg_ref, 1.0)
        qn = head_norm(0, jj, qg_ref, scale)
        for half in range(2):
            j = 2 * jj + half
            hl = slice(half * LANES, (half + 1) * LANES)
            ok_ref[:, 2 * j * LANES:(2 * j + 1) * LANES] = kn[:, hl]
            ok_ref[:, (2 * j + 1) * LANES:(2 * j + 2) * LANES] = gk[:, j * LANES:(j + 1) * LANES]
            qn_t = _dot_nt(eye_ref[...], qn[:, hl]).astype(BF16)
            gq_t = _dot_nt(eye_ref[...], gq[:, j * LANES:(j + 1) * LANES]).astype(BF16)
            for qq in range(ts // tq):
                qt_ref[j, qq, 0:LANES, :] = qn_t[:, qq * tq:(qq + 1) * tq]
                qt_ref[j, qq, LANES:, :] = gq_t[:, qq * tq:(qq + 1) * tq]


def _gate_placements(d):
    n_pairs = d // LANES
    pq = np.zeros((4 * LANES, d), np.float32)
    pk = np.zeros((4 * LANES, d), np.float32)
    ones_row = GATE_PARTS * LANES
    for pair in range(n_pairs):
        for which in range(2):
            head = 2 * pair + which
            base = pair * LANES + which * 2 * GATE_PARTS
            for p in range(GATE_PARTS):
                pq[p * LANES + head, base + p] = 1.0
                pk[ones_row, base + p] = 1.0
                pq[ones_row, base + GATE_PARTS + p] = 1.0
                pk[p * LANES + head, base + GATE_PARTS + p] = -1.0
    return pq, pk


def _fox_prep(proj, flog, b_f, qg, kg, *, batch, seq_len, ts, tq, tk, name):
    t = proj.shape[0]
    n_heads = b_f.shape[0]
    d = n_heads * HEAD_DIM
    assert seq_len % ts == 0 and ts % tk == 0 and ts % tq == 0 and n_heads <= LANES
    assert ts % CUM_BLOCK == 0 and d % (2 * LANES) == 0
    ns = seq_len // ts
    n_pairs = d // LANES
    pq, pk = _gate_placements(d)
    hs = np.kron(np.eye(2 * LANES // HEAD_DIM, dtype=np.float32), np.ones((HEAD_DIM, HEAD_DIM), np.float32))
    tri = np.tril(np.ones((CUM_BLOCK, CUM_BLOCK), np.float32))
    const = lambda b, s: (0, 0)
    rows = lambda b, s: (b * ns + s, 0)
    return pl.pallas_call(
        functools.partial(_fox_prep_kernel, ts=ts, tq=tq, tk=tk, d=d, scale=HEAD_DIM ** -0.5 * LOG2_E),
        grid=(batch, ns),
        in_specs=[
            pl.BlockSpec((ts, 2 * d), rows),
            pl.BlockSpec((ts, d), lambda b, s: (b * ns + s, 2)),
            pl.BlockSpec((ts, LANES), rows),
            pl.BlockSpec((1, LANES), const),
            pl.BlockSpec((1, d), const),
            pl.BlockSpec((1, d), const),
            pl.BlockSpec((2 * LANES, 2 * LANES), const),
            pl.BlockSpec((CUM_BLOCK, CUM_BLOCK), const),
            pl.BlockSpec((4 * LANES, d), const),
            pl.BlockSpec((4 * LANES, d), const),
            pl.BlockSpec((LANES, LANES), const),
        ],
        out_specs=[
            pl.BlockSpec((None, n_pairs, ts // tq, 2 * LANES, tq), lambda b, s: (b, 0, s, 0, 0)),
            pl.BlockSpec((ts, 2 * d), rows),
            pl.BlockSpec((None, n_pairs, ts // tk, VT_ROWS, tk), lambda b, s: (b, 0, s, 0, 0)),
        ],
        out_shape=[
            jax.ShapeDtypeStruct((batch, n_pairs, seq_len // tq, 2 * LANES, tq), BF16),
            jax.ShapeDtypeStruct((t, 2 * d), BF16),
            jax.ShapeDtypeStruct((batch, n_pairs, seq_len // tk, VT_ROWS, tk), BF16),
        ],
        scratch_shapes=[pltpu.VMEM((1, LANES), F32)],
        compiler_params=_cparams("parallel", "arbitrary"),
        name=name,
    )(proj, proj, flog,
      jnp.pad(b_f.astype(F32), (0, LANES - n_heads)).reshape(1, LANES),
      jnp.tile(qg, n_heads).reshape(1, d), jnp.tile(kg, n_heads).reshape(1, d),
      jnp.asarray(hs, BF16), jnp.asarray(tri, BF16), jnp.asarray(pq, BF16), jnp.asarray(pk, BF16),
      jnp.eye(LANES, dtype=BF16))


def _fox_attn_kernel(qt_ref, k_ref, vt_ref, gate_ref, o_ref, m_ref, acc_ref, s_ref, smax_ref, *, tq, tk):
    qi = pl.program_id(2)
    lane = lax.broadcasted_iota(jnp.int32, (2 * LANES, 1), 0)
    g0 = LANES
    head_lanes = (
        (lane < HEAD_DIM) | ((lane >= g0) & (lane < g0 + 2 * GATE_PARTS)),
        ((lane >= HEAD_DIM) & (lane < LANES)) | ((lane >= g0 + 2 * GATE_PARTS) & (lane < g0 + 4 * GATE_PARTS)),
    )
    qt = qt_ref[...]
    q2t = jnp.concatenate([jnp.where(sel, qt, jnp.zeros_like(qt)) for sel in head_lanes], axis=1)

    m_ref[...] = jnp.full_like(m_ref, -jnp.inf)
    acc_ref[...] = jnp.zeros_like(acc_ref)

    def scores(slot, j):
        kb = k_ref[pl.ds(pl.multiple_of(j * tk, tk), tk), :]
        s = _dot(kb, q2t)
        s_ref[slot] = s
        smax_ref[slot] = jnp.broadcast_to(jnp.max(s, axis=0, keepdims=True), smax_ref.shape[1:])

    def update(slot, j, diag):
        s = s_ref[slot]
        if diag is None:
            s_max = smax_ref[slot]
        else:
            key = lax.broadcasted_iota(jnp.int32, (tk, 2 * tq), 0) + diag
            qry = lax.broadcasted_iota(jnp.int32, (tk, 2 * tq), 1)
            s = jnp.where(jnp.where(qry >= tq, qry - tq, qry) >= key, s, -jnp.inf)
            s_max = jnp.max(s, axis=0, keepdims=True)
        m_prev = m_ref[slot]
        m_new = jnp.maximum(m_prev, s_max)
        m_safe = jnp.where(m_new == -jnp.inf, 0.0, m_new)
        p = jnp.exp2(s - m_safe[0:1, :])
        alpha = jnp.exp2(m_prev - m_safe)
        acc_ref[slot] = alpha[0:1, :] * acc_ref[slot] + _dot(vt_ref[j], p.astype(BF16))
        m_ref[slot] = m_new

    scores(0, 0)

    def pair(i):
        scores(1, 2 * i + 1)
        update(0, 2 * i, None)
        scores(0, 2 * i + 2)
        update(1, 2 * i + 1, None)

    def two_pairs(i, carry):
        pair(2 * i)
        pair(2 * i + 1)
        return carry

    assert KEY_BLOCKS_PER_QUERY_BLOCK == 4 and tq == 4 * tk
    lax.fori_loop(0, qi, two_pairs, 0)
    j0 = KEY_BLOCKS_PER_QUERY_BLOCK * qi
    scores(1, j0 + 1)
    update(0, j0, 0)
    scores(0, j0 + 2)
    update(1, j0 + 1, tk)
    scores(1, j0 + 3)
    update(0, j0 + 2, 2 * tk)
    update(1, j0 + 3, 3 * tk)

    m = jnp.maximum(m_ref[0], m_ref[1])
    w0 = jnp.exp2(m_ref[0] - m)
    w1 = jnp.exp2(m_ref[1] - m)
    acc = w0[0:1, :] * acc_ref[0] + w1[0:1, :] * acc_ref[1]
    o_t = jnp.concatenate([acc[0:HEAD_DIM, 0:tq] / acc[LANES:LANES + 1, 0:tq],
                           acc[HEAD_DIM:LANES, tq:] / acc[LANES:LANES + 1, tq:]], axis=0)
    o_ref[...] = (o_t.T * _sigmoid(gate_ref[...].astype(F32))).astype(o_ref.dtype)


def _fox_attn(qt, kp, vt, proj, *, batch, seq_len, tq, tk, name):
    t = kp.shape[0]
    d = kp.shape[1] // 2
    n_pairs = d // LANES
    nq = seq_len // tq
    nk = seq_len // tk
    assert seq_len % tq == 0 and tq == KEY_BLOCKS_PER_QUERY_BLOCK * tk and tk % LANES == 0
    assert qt.shape == (batch, n_pairs, nq, 2 * LANES, tq) and vt.shape == (batch, n_pairs, nk, VT_ROWS, tk)
    g_blk = 3 * d // LANES
    return pl.pallas_call(
        functools.partial(_fox_attn_kernel, tq=tq, tk=tk),
        grid=(batch, n_pairs, nq),
        in_specs=[
            pl.BlockSpec((None, None, None, 2 * LANES, tq), lambda b, p, i: (b, p, i, 0, 0)),
            pl.BlockSpec((seq_len, 2 * LANES), lambda b, p, i: (b, p)),
            pl.BlockSpec((None, None, nk, VT_ROWS, tk), lambda b, p, i: (b, p, 0, 0, 0)),
            pl.BlockSpec((tq, LANES), lambda b, p, i: (b * nq + i, g_blk + p)),
        ],
        out_specs=pl.BlockSpec((tq, LANES), lambda b, p, i: (b * nq + i, p)),
        out_shape=jax.ShapeDtypeStruct((t, d), BF16),
        scratch_shapes=[
            pltpu.VMEM((2, SUBLANES, 2 * tq), F32),
            pltpu.VMEM((2, VT_ROWS, 2 * tq), F32),
            pltpu.VMEM((2, tk, 2 * tq), F32),
            pltpu.VMEM((2, SUBLANES, 2 * tq), F32),
        ],
        compiler_params=_cparams("parallel", "parallel", "arbitrary"),
        name=name,
    )(qt, kp, vt, proj)


def _tile(n, pref):
    tile = min(n, pref)
    while n % tile:
        tile //= 2
    return tile


def kernel(x, mix_norm_g, ffn_norm_g, ssd_w_in, ssd_conv_w, ssd_conv_b, ssd_dt_bias, ssd_a_log, ssd_d,
           ssd_norm_g, ssd_w_out, fox_w_in, fox_b_f, fox_q_norm_g, fox_k_norm_g, fox_w_out,
           ffn_w_up, ffn_conv_w, ffn_conv_b, ffn_w_down, final_norm_g):
    batch, seq_len, d_model = x.shape
    t = batch * seq_len
    depth = mix_norm_g.shape[0]
    tm_proj = _tile(seq_len, 512)
    tm_ffn = _tile(seq_len, 1024)
    tm_out = _tile(seq_len, 1024)
    ts_ssd = _tile(seq_len, 512)
    tq = _tile(seq_len, 1024)
    ts_prep = tq

    def lane_pad(w):
        return jnp.pad(w, ((0, 0), (0, LANES - w.shape[1])))

    h = x.reshape(t, d_model)
    for i in range(depth):
        j = i // 2
        if i % 2 == 0:
            w_in = ssd_w_in[j]
            n_heads = ssd_d.shape[1]
            n_main = w_in.shape[1] - n_heads
            proj, dt_raw = _ssd_in_proj(
                h, mix_norm_g[i], w_in[:, :n_main].astype(BF16), lane_pad(w_in[:, n_main:]).astype(BF16),
                ssd_conv_w[j], ssd_conv_b[j], d_inner=n_heads * HEAD_DIM,
                tm=tm_proj, chunk=512, seq_len=seq_len, name=f"ssd_in_proj_{j}")
            y = _ssd_core(proj, dt_raw, ssd_dt_bias[j], ssd_a_log[j], ssd_d[j], ssd_norm_g[j],
                          batch=batch, seq_len=seq_len, ts=ts_ssd, name=f"ssd_core_{j}")
            h = _matmul_residual(y, ssd_w_out[j].astype(BF16), h, tm=tm_out, name=f"ssd_out_proj_{j}",
                                 in_place=i > 0)
        else:
            w_in = fox_w_in[j]
            n_heads = fox_b_f.shape[1]
            n_main = w_in.shape[1] - n_heads
            proj, flog = _norm_matmul(
                h, mix_norm_g[i], w_in[:, :n_main].astype(BF16), lane_pad(w_in[:, n_main:]).astype(BF16),
                tm=tm_proj, chunk=512, name=f"fox_in_proj_{j}")
            qt, kp, vt = _fox_prep(proj, flog, fox_b_f[j], fox_q_norm_g[j], fox_k_norm_g[j],
                                   batch=batch, seq_len=seq_len, ts=ts_prep, tq=tq, tk=tq // KEY_BLOCKS_PER_QUERY_BLOCK,
                                   name=f"fox_prep_{j}")
            o = _fox_attn(qt, kp, vt, proj, batch=batch, seq_len=seq_len, tq=tq, tk=tq // KEY_BLOCKS_PER_QUERY_BLOCK,
                          name=f"fox_attn_{j}")
            h = _matmul_residual(o, fox_w_out[j].astype(BF16), h, tm=tm_out, name=f"fox_out_proj_{j}")
        act = _ffn_up(h, ffn_norm_g[i], ffn_w_up[i].astype(BF16), ffn_conv_w[i], ffn_conv_b[i],
                      tm=tm_ffn, chunk=256, seq_len=seq_len, name=f"ffn_up_{i}")
        h = _matmul_residual(act, ffn_w_down[i].astype(BF16), h, final_norm_g if i == depth - 1 else None,
                             tm=tm_out, name=f"ffn_down_{i}")
    return h.reshape(batch, seq_len, d_model)
```

```python
import functools

import numpy as np
import jax
import jax.numpy as jnp
from jax import lax
from jax.experimental import pallas as pl
from jax.experimental.pallas import tpu as pltpu

F32 = jnp.float32
BF16 = jnp.bfloat16

EPS = 1e-6
LOG2_E = 1.4426950408889634
LANES = 128
SUBLANES = 8
HEAD_DIM = 64
SSD_CHUNK = 128
SSD_GROUPS = 4
SSD_STATE = 128
SSD_CONV = 4
FFN_CONV = 3
VMEM_LIMIT_BYTES = 56 * 1024 * 1024


def _cparams(*sem):
    return pltpu.CompilerParams(dimension_semantics=sem, vmem_limit_bytes=VMEM_LIMIT_BYTES)


def _sigmoid(x):
    return 1.0 / (1.0 + jnp.exp(-x))


def _softplus(x):
    return jnp.maximum(x, 0.0) + jnp.log1p(jnp.exp(-jnp.abs(x)))


def _split_bf16(v, parts):
    out = []
    for _ in range(parts - 1):
        p = v.astype(BF16)
        out.append(p)
        v = v - p.astype(F32)
    out.append(v.astype(BF16))
    return out


def _dot(a, b):
    return jnp.dot(a, b, preferred_element_type=F32)


def _dot_nt(a, b):
    return lax.dot_general(a, b, (((1,), (1,)), ((), ())), preferred_element_type=F32)


def _select_dot(sel, v, parts):
    acc = None
    for p in _split_bf16(v, parts):
        t = _dot(sel, p)
        acc = t if acc is None else acc + t
    return acc


def _rms_normed(x, g):
    ms = jnp.mean(x * x, axis=-1, keepdims=True)
    return (x * lax.rsqrt(ms + EPS) * g).astype(BF16)


def _resident(shape, layer=None):
    if layer is None:
        return pl.BlockSpec(shape, lambda *_: (0,) * len(shape), pipeline_mode=pl.Buffered(1))
    return pl.BlockSpec((None,) + tuple(shape), lambda *_: (layer,) + (0,) * len(shape),
                        pipeline_mode=pl.Buffered(1))


def _norm_matmul_kernel(x_ref, g_ref, w_ref, wa_ref, o_ref, oa_ref, *, chunk):
    xn = _rms_normed(x_ref[...], g_ref[...])
    oa_ref[...] = _dot(xn, wa_ref[...])
    for c in range(w_ref.shape[1] // chunk):
        sl = slice(c * chunk, (c + 1) * chunk)
        o_ref[:, sl] = _dot(xn, w_ref[:, sl]).astype(o_ref.dtype)


def _norm_matmul(x, g, w_stack, layer, n, w_aux, *, tm, chunk, name):
    t, d = x.shape
    assert t % tm == 0 and n % chunk == 0 and n % LANES == 0
    return pl.pallas_call(
        functools.partial(_norm_matmul_kernel, chunk=chunk),
        grid=(t // tm,),
        in_specs=[
            pl.BlockSpec((tm, d), lambda i: (i, 0)),
            _resident((1, d)),
            _resident((d, n), layer),
            _resident((d, LANES)),
        ],
        out_specs=[pl.BlockSpec((tm, n), lambda i: (i, 0)), pl.BlockSpec((tm, LANES), lambda i: (i, 0))],
        out_shape=[jax.ShapeDtypeStruct((t, n), BF16), jax.ShapeDtypeStruct((t, LANES), F32)],
        compiler_params=_cparams("parallel"),
        name=name,
    )(x, g.reshape(1, d), w_stack, w_aux)


def _matmul_residual_kernel(*refs, final):
    if final:
        y_ref, w_ref, h_ref, fg_ref, o_ref = refs
    else:
        y_ref, w_ref, h_ref, o_ref = refs
    out = h_ref[...] + _dot(y_ref[...], w_ref[...])
    if final:
        ms = jnp.mean(out * out, axis=-1, keepdims=True)
        out = out * lax.rsqrt(ms + EPS) * fg_ref[...]
    o_ref[...] = out


def _matmul_residual(y, w_stack, layer, h, final_g=None, *, tm, name, in_place=True):
    t, k = y.shape
    d = w_stack.shape[2]
    assert t % tm == 0 and w_stack.shape[1] == k
    final = final_g is not None
    in_specs = [
        pl.BlockSpec((tm, k), lambda i: (i, 0)),
        _resident((k, d), layer),
        pl.BlockSpec((tm, d), lambda i: (i, 0)),
    ]
    args = [y, w_stack, h]
    if final:
        in_specs.append(_resident((1, d)))
        args.append(final_g.reshape(1, d))
    return pl.pallas_call(
        functools.partial(_matmul_residual_kernel, final=final),
        grid=(t // tm,),
        in_specs=in_specs,
        out_specs=pl.BlockSpec((tm, d), lambda i: (i, 0)),
        out_shape=jax.ShapeDtypeStruct((t, d), F32),
        input_output_aliases={2: 0} if in_place else {},
        compiler_params=_cparams("parallel"),
        name=name,
    )(*args)


FFN_HALO = 16


def _ffn_up_kernel(x_ref, xh_ref, g_ref, w_ref, cw_ref, cb_ref, o_ref, xn_ref, buf_ref, *, tm, f, chunk, seq_len):
    seq_start = (pl.program_id(0) * tm) % seq_len == 0
    xn_ref[0:tm, :] = _rms_normed(x_ref[...], g_ref[...])
    xn_ref[tm:, :] = _rms_normed(xh_ref[...], g_ref[...])
    for c in range(f // chunk):
        sl = slice(c * chunk, (c + 1) * chunk)
        buf = buf_ref.at[c % 2]
        gate = _dot(xn_ref[...], w_ref[:, sl])
        up = _dot(xn_ref[0:tm, :], w_ref[:, f + c * chunk:f + (c + 1) * chunk])
        buf[0:SUBLANES, :] = jnp.where(seq_start, 0.0, gate[tm + FFN_HALO - SUBLANES:, :])
        buf[SUBLANES:, :] = gate[0:tm, :]
        conv = cb_ref[:, sl]
        for k in range(FFN_CONV):
            off = SUBLANES - (FFN_CONV - 1) + k
            conv = conv + buf[off:off + tm, :] * cw_ref[k:k + 1, sl]
        sig = 1.0 / (1.0 + jnp.exp2(conv * (-LOG2_E)))
        o_ref[:, sl] = (conv * sig * up).astype(o_ref.dtype)


def _ffn_up(h, g, w_stack, layer, conv_w, conv_b, *, tm, chunk, seq_len, name):
    t, d = h.shape
    f = w_stack.shape[2] // 2
    assert t % tm == 0 and seq_len % tm == 0 and tm % FFN_HALO == 0 and f % chunk == 0
    return pl.pallas_call(
        functools.partial(_ffn_up_kernel, tm=tm, f=f, chunk=chunk, seq_len=seq_len),
        grid=(t // tm,),
        in_specs=[
            pl.BlockSpec((tm, d), lambda i: (i, 0)),
            pl.BlockSpec((FFN_HALO, d), lambda i: (jnp.maximum(i * (tm // FFN_HALO) - 1, 0), 0)),
            _resident((1, d)),
            _resident((d, 2 * f), layer),
            _resident((FFN_CONV, f)),
            _resident((1, f)),
        ],
        out_specs=pl.BlockSpec((tm, f), lambda i: (i, 0)),
        out_shape=jax.ShapeDtypeStruct((t, f), BF16),
        scratch_shapes=[
            pltpu.VMEM((tm + FFN_HALO, d), BF16),
            pltpu.VMEM((2, tm + SUBLANES, chunk), F32),
        ],
        compiler_params=_cparams("parallel"),
        name=name,
    )(h, h, g.reshape(1, d), w_stack, conv_w, conv_b.reshape(1, f))


def _ssd_in_proj_kernel(x_ref, xh_ref, g_ref, w_ref, wa_ref, cw_ref, cb_ref, o_ref, oa_ref, xn_ref, buf_ref,
                        *, tm, d_inner, chunk, seq_len):
    seq_start = (pl.program_id(0) * tm) % seq_len == 0
    xn_ref[0:tm, :] = _rms_normed(x_ref[...], g_ref[...])
    xn_ref[tm:, :] = _rms_normed(xh_ref[...], g_ref[...])
    oa_ref[...] = _dot(xn_ref[0:tm, :], wa_ref[...])
    for c in range(w_ref.shape[1] // chunk):
        sl = slice(c * chunk, (c + 1) * chunk)
        if c * chunk < d_inner:
            pre = _dot(xn_ref[0:tm, :], w_ref[:, sl])
        else:
            csl = slice(c * chunk - d_inner, (c + 1) * chunk - d_inner)
            buf = buf_ref.at[c % 2]
            ext = _dot(xn_ref[...], w_ref[:, sl])
            buf[0:SUBLANES, :] = jnp.where(seq_start, 0.0, ext[tm + FFN_HALO - SUBLANES:, :])
            buf[SUBLANES:, :] = ext[0:tm, :]
            pre = cb_ref[:, csl]
            for k in range(SSD_CONV):
                off = SUBLANES - (SSD_CONV - 1) + k
                pre = pre + buf[off:off + tm, :] * cw_ref[k:k + 1, csl]
        o_ref[:, sl] = (pre / (1.0 + jnp.exp2(pre * (-LOG2_E)))).astype(o_ref.dtype)


def _ssd_in_proj(h, g, w_stack, layer, n, w_aux, conv_w, conv_b, *, d_inner, tm, chunk, seq_len, name):
    t, d = h.shape
    n_conv = n - d_inner
    assert t % tm == 0 and seq_len % tm == 0 and tm % FFN_HALO == 0
    assert n % chunk == 0 and d_inner % chunk == 0 and conv_w.shape == (SSD_CONV, n_conv)
    return pl.pallas_call(
        functools.partial(_ssd_in_proj_kernel, tm=tm, d_inner=d_inner, chunk=chunk, seq_len=seq_len),
        grid=(t // tm,),
        in_specs=[
            pl.BlockSpec((tm, d), lambda i: (i, 0)),
            pl.BlockSpec((FFN_HALO, d), lambda i: (jnp.maximum(i * (tm // FFN_HALO) - 1, 0), 0)),
            _resident((1, d)),
            _resident((d, n), layer),
            _resident((d, LANES)),
            _resident((SSD_CONV, n_conv)),
            _resident((1, n_conv)),
        ],
        out_specs=[pl.BlockSpec((tm, n), lambda i: (i, 0)), pl.BlockSpec((tm, LANES), lambda i: (i, 0))],
        out_shape=[jax.ShapeDtypeStruct((t, n), BF16), jax.ShapeDtypeStruct((t, LANES), F32)],
        scratch_shapes=[
            pltpu.VMEM((tm + FFN_HALO, d), BF16),
            pltpu.VMEM((2, tm + SUBLANES, chunk), F32),
        ],
        compiler_params=_cparams("parallel"),
        name=name,
    )(h, h, g.reshape(1, d), w_stack, w_aux, conv_w, conv_b.reshape(1, n_conv))


def _ssd_kernel(z_ref, x_ref, bc_ref, dt_ref, dtb_ref, alog_ref, dsk_ref, ng_ref,
                e_ref, tri_ref, o_ref, state_ref, *, ts):
    d_inner = x_ref.shape[1]
    gw = d_inner // SSD_GROUPS
    gn = SSD_GROUPS * SSD_STATE

    @pl.when(pl.program_id(1) == 0)
    def _():
        state_ref[...] = jnp.zeros_like(state_ref)

    dt_all = _softplus(dt_ref[...] + dtb_ref[...])
    da_all = dt_all * (-jnp.exp(alog_ref[...]))

    ll = SSD_CHUNK
    row = lax.broadcasted_iota(jnp.int32, (ll, ll), 0)
    col = lax.broadcasted_iota(jnp.int32, (ll, ll), 1)
    causal = row >= col
    first_head = lax.broadcasted_iota(jnp.int32, (ll, LANES), 1) < HEAD_DIM
    expand = e_ref[...]

    for c in range(ts // ll):
        r0 = c * ll
        xs = x_ref[r0:r0 + ll, :].astype(F32)
        bc = bc_ref[r0:r0 + ll, :]
        dt = dt_all[r0:r0 + ll]
        acs = _select_dot(tri_ref[...], da_all[r0:r0 + ll], 3)
        acs_t = acs.T
        tot = acs[ll - 1:ll, :]
        tile16 = (2 * SUBLANES, LANES)
        per_head = jnp.concatenate(
            [jnp.exp(acs).astype(BF16), dt.astype(BF16), jnp.exp(tot - acs).astype(BF16)]
            + [jnp.broadcast_to(p, tile16) for p in _split_bf16(jnp.exp(tot), 3)], axis=0)
        per_chan = _dot(per_head, expand)
        ea_x = per_chan[0:ll]
        dt_x = per_chan[ll:2 * ll]
        ds_x = per_chan[2 * ll:3 * ll]
        r1 = 3 * ll
        chunk_decay_x = (per_chan[r1:r1 + 1] + per_chan[r1 + 16:r1 + 17]) + per_chan[r1 + 32:r1 + 33]
        xdt = xs * dt_x
        x_state = (xdt * ds_x).astype(BF16)

        ys = []
        for g in range(SSD_GROUPS):
            b_g = bc[:, g * SSD_STATE:(g + 1) * SSD_STATE]
            c_g = bc[:, gn + g * SSD_STATE:gn + (g + 1) * SSD_STATE]
            cb = _dot_nt(c_g, b_g)
            st = state_ref[g]
            y_off = _dot(c_g, st.astype(BF16)) * ea_x[:, g * gw:(g + 1) * gw]
            y_diag = []
            for pr in range(gw // LANES):
                h0 = (g * gw + pr * LANES) // HEAD_DIM
                ms = []
                for hh in (h0, h0 + 1):
                    diff = acs[:, hh:hh + 1] - acs_t[hh:hh + 1, :]
                    dec = jnp.exp(jnp.where(causal, diff, -jnp.inf))
                    ms.append((cb * dec).astype(BF16))
                lhs = jnp.concatenate(ms, axis=1)
                xp = xdt[:, g * gw + pr * LANES:g * gw + (pr + 1) * LANES]
                rhs = jnp.concatenate([jnp.where(first_head, xp, 0.0),
                                       jnp.where(first_head, 0.0, xp)], axis=0).astype(BF16)
                y_diag.append(_dot(lhs, rhs))
            ys.append(jnp.concatenate(y_diag, axis=1) + y_off)
            new = _dot(b_g.astype(F32).T.astype(BF16), x_state[:, g * gw:(g + 1) * gw])
            state_ref[g] = st * chunk_decay_x[:, g * gw:(g + 1) * gw] + new

        y = jnp.concatenate(ys, axis=1) + xs * dsk_ref[...]
        yz = y * z_ref[r0:r0 + ll, :].astype(F32)
        outs = []
        for g in range(SSD_GROUPS):
            blk = yz[:, g * gw:(g + 1) * gw]
            ms = jnp.mean(blk * blk, axis=-1, keepdims=True)
            outs.append(blk * lax.rsqrt(ms + EPS))
        o_ref[r0:r0 + ll, :] = (jnp.concatenate(outs, axis=1) * ng_ref[...]).astype(o_ref.dtype)


def _ssd_core(proj, dt_raw, dt_bias, a_log, d_skip, norm_g, *, batch, seq_len, ts, name):
    t = proj.shape[0]
    n_heads = d_skip.shape[0]
    d_inner = n_heads * HEAD_DIM
    gn = SSD_GROUPS * SSD_STATE
    assert proj.shape[1] == 2 * d_inner + 2 * gn and (2 * d_inner) % (2 * gn) == 0
    assert seq_len % ts == 0 and ts % SSD_CHUNK == 0 and n_heads <= LANES
    ns = seq_len // ts
    bc_blk = 2 * d_inner // (2 * gn)

    def pad_lanes(v):
        return jnp.pad(v.astype(F32), (0, LANES - v.shape[0])).reshape(1, LANES)

    expand = np.zeros((LANES, d_inner), np.float32)
    expand[np.arange(d_inner) // HEAD_DIM, np.arange(d_inner)] = 1.0
    tri = np.tril(np.ones((SSD_CHUNK, SSD_CHUNK), np.float32))

    def rows(b, s):
        return b * ns + s

    return pl.pallas_call(
        functools.partial(_ssd_kernel, ts=ts),
        grid=(batch, ns),
        in_specs=[
            pl.BlockSpec((ts, d_inner), lambda b, s: (rows(b, s), 0)),
            pl.BlockSpec((ts, d_inner), lambda b, s: (rows(b, s), 1)),
            pl.BlockSpec((ts, 2 * gn), lambda b, s: (rows(b, s), bc_blk)),
            pl.BlockSpec((ts, LANES), lambda b, s: (rows(b, s), 0)),
            _resident((1, LANES)),
            _resident((1, LANES)),
            _resident((1, d_inner)),
            _resident((1, d_inner)),
            _resident((LANES, d_inner)),
            _resident((SSD_CHUNK, SSD_CHUNK)),
        ],
        out_specs=pl.BlockSpec((ts, d_inner), lambda b, s: (rows(b, s), 0)),
        out_shape=jax.ShapeDtypeStruct((t, d_inner), BF16),
        scratch_shapes=[pltpu.VMEM((SSD_GROUPS, SSD_STATE, d_inner // SSD_GROUPS), F32)],
        compiler_params=_cparams("parallel", "arbitrary"),
        name=name,
    )(proj, proj, proj, dt_raw,
      pad_lanes(dt_bias), pad_lanes(a_log),
      jnp.repeat(d_skip, HEAD_DIM).reshape(1, d_inner), norm_g.reshape(1, d_inner),
      jnp.asarray(expand, BF16), jnp.asarray(tri, BF16))


GATE_PARTS = 3
VT_ROWS = LANES + 16
KEY_BLOCKS_PER_QUERY_BLOCK = 4
CUM_BLOCK = 256


def _fox_prep_kernel(qk_ref, v_ref, fl_ref, bf_ref, qg_ref, kg_ref, hs_ref, tri_ref, pq_ref, pk_ref,
                     qt_ref, ok_ref, vt_ref, carry_ref, *, ts, tq, tk, d, scale):
    @pl.when(pl.program_id(1) == 0)
    def _():
        carry_ref[...] = jnp.zeros_like(carry_ref)

    for j in range(d // LANES):
        for kk in range(ts // tk):
            vj = v_ref[kk * tk:(kk + 1) * tk, j * LANES:(j + 1) * LANES]
            vt_ref[j, kk, 0:LANES, :] = vj.astype(F32).T.astype(BF16)
            vt_ref[j, kk, LANES:, :] = jnp.ones((VT_ROWS - LANES, tk), BF16)

    logit = fl_ref[...] + bf_ref[...]
    log_f = jnp.minimum(logit, 0.0) - jnp.log1p(jnp.exp(-jnp.abs(logit)))
    carry = carry_ref[...]
    cum_blocks = []
    for r in range(ts // CUM_BLOCK):
        c = _select_dot(tri_ref[...], log_f[r * CUM_BLOCK:(r + 1) * CUM_BLOCK], 3) + carry
        cum_blocks.append(c)
        carry = c[CUM_BLOCK - 1:CUM_BLOCK, :]
    carry_ref[...] = carry
    cum = jnp.concatenate(cum_blocks, axis=0)
    pieces = _split_bf16(cum * LOG2_E, GATE_PARTS) + [jnp.ones((ts, LANES), BF16)]
    feats = jnp.concatenate(pieces, axis=1)
    gq = _dot(feats, pq_ref[...]).astype(BF16)
    gk = _dot(feats, pk_ref[...]).astype(BF16)

    def head_norm(base, jj, g_ref, mult):
        sl = slice(2 * jj * LANES, (2 * jj + 2) * LANES)
        v = qk_ref[:, base + 2 * jj * LANES:base + (2 * jj + 2) * LANES].astype(F32)
        ssq = _dot((v * v).astype(BF16), hs_ref[...])
        vn = v * lax.rsqrt(ssq * (1.0 / HEAD_DIM) + EPS) * (g_ref[:, sl] * mult)
        return vn.astype(BF16)

    for jj in range(d // (2 * LANES)):
        kn = head_norm(d, jj, kg_ref, 1.0)
        qn = head_norm(0, jj, qg_ref, scale)
        for half in range(2):
            j = 2 * jj + half
            hl = slice(half * LANES, (half + 1) * LANES)
            ok_ref[:, 2 * j * LANES:(2 * j + 1) * LANES] = kn[:, hl]
            ok_ref[:, (2 * j + 1) * LANES:(2 * j + 2) * LANES] = gk[:, j * LANES:(j + 1) * LANES]
            qn_t = qn[:, hl].astype(F32).T.astype(BF16)
            gq_t = gq[:, j * LANES:(j + 1) * LANES].astype(F32).T.astype(BF16)
            for qq in range(ts // tq):
                qt_ref[j, qq, 0:LANES, :] = qn_t[:, qq * tq:(qq + 1) * tq]
                qt_ref[j, qq, LANES:, :] = gq_t[:, qq * tq:(qq + 1) * tq]


def _gate_placements(d):
    n_pairs = d // LANES
    pq = np.zeros((4 * LANES, d), np.float32)
    pk = np.zeros((4 * LANES, d), np.float32)
    ones_row = GATE_PARTS * LANES
    for pair in range(n_pairs):
        for which in range(2):
            head = 2 * pair + which
            base = pair * LANES + which * 2 * GATE_PARTS
            for p in range(GATE_PARTS):
                pq[p * LANES + head, base + p] = 1.0
                pk[ones_row, base + p] = 1.0
                pq[ones_row, base + GATE_PARTS + p] = 1.0
                pk[p * LANES + head, base + GATE_PARTS + p] = -1.0
    return pq, pk


def _fox_prep(proj, flog, b_f, qg, kg, *, batch, seq_len, ts, tq, tk, name):
    t = proj.shape[0]
    n_heads = b_f.shape[0]
    d = n_heads * HEAD_DIM
    assert seq_len % ts == 0 and ts % tk == 0 and ts % tq == 0 and n_heads <= LANES
    assert ts % CUM_BLOCK == 0 and d % (2 * LANES) == 0
    ns = seq_len // ts
    n_pairs = d // LANES
    pq, pk = _gate_placements(d)
    hs = np.kron(np.eye(2 * LANES // HEAD_DIM, dtype=np.float32), np.ones((HEAD_DIM, HEAD_DIM), np.float32))
    tri = np.tril(np.ones((CUM_BLOCK, CUM_BLOCK), np.float32))
    const = lambda b, s: (0, 0)
    rows = lambda b, s: (b * ns + s, 0)
    return pl.pallas_call(
        functools.partial(_fox_prep_kernel, ts=ts, tq=tq, tk=tk, d=d, scale=HEAD_DIM ** -0.5 * LOG2_E),
        grid=(batch, ns),
        in_specs=[
            pl.BlockSpec((ts, 2 * d), rows),
            pl.BlockSpec((ts, d), lambda b, s: (b * ns + s, 2)),
            pl.BlockSpec((ts, LANES), rows),
            pl.BlockSpec((1, LANES), const),
            pl.BlockSpec((1, d), const),
            pl.BlockSpec((1, d), const),
            pl.BlockSpec((2 * LANES, 2 * LANES), const),
            pl.BlockSpec((CUM_BLOCK, CUM_BLOCK), const),
            pl.BlockSpec((4 * LANES, d), const),
            pl.BlockSpec((4 * LANES, d), const),
        ],
        out_specs=[
            pl.BlockSpec((None, n_pairs, ts // tq, 2 * LANES, tq), lambda b, s: (b, 0, s, 0, 0)),
            pl.BlockSpec((ts, 2 * d), rows),
            pl.BlockSpec((None, n_pairs, ts // tk, VT_ROWS, tk), lambda b, s: (b, 0, s, 0, 0)),
        ],
        out_shape=[
            jax.ShapeDtypeStruct((batch, n_pairs, seq_len // tq, 2 * LANES, tq), BF16),
            jax.ShapeDtypeStruct((t, 2 * d), BF16),
            jax.ShapeDtypeStruct((batch, n_pairs, seq_len // tk, VT_ROWS, tk), BF16),
        ],
        scratch_shapes=[pltpu.VMEM((1, LANES), F32)],
        compiler_params=_cparams("parallel", "arbitrary"),
        name=name,
    )(proj, proj, flog,
      jnp.pad(b_f.astype(F32), (0, LANES - n_heads)).reshape(1, LANES),
      jnp.tile(qg, n_heads).reshape(1, d), jnp.tile(kg, n_heads).reshape(1, d),
      jnp.asarray(hs, BF16), jnp.asarray(tri, BF16), jnp.asarray(pq, BF16), jnp.asarray(pk, BF16))


def _fox_attn_kernel(qt_ref, k_ref, vt_ref, gate_ref, o_ref, m_ref, acc_ref, s_ref, smax_ref, *, tq, tk):
    qi = pl.program_id(2)
    lane = lax.broadcasted_iota(jnp.int32, (2 * LANES, 1), 0)
    g0 = LANES
    head_lanes = (
        (lane < HEAD_DIM) | ((lane >= g0) & (lane < g0 + 2 * GATE_PARTS)),
        ((lane >= HEAD_DIM) & (lane < LANES)) | ((lane >= g0 + 2 * GATE_PARTS) & (lane < g0 + 4 * GATE_PARTS)),
    )
    qt = qt_ref[...]
    q2t = jnp.concatenate([jnp.where(sel, qt, jnp.zeros_like(qt)) for sel in head_lanes], axis=1)

    m_ref[...] = jnp.full_like(m_ref, -jnp.inf)
    acc_ref[...] = jnp.zeros_like(acc_ref)

    def key_block(j):
        return k_ref[pl.ds(pl.multiple_of(j * tk, tk), tk), :]

    def scores(slot, j):
        s = _dot(key_block(j), q2t)
        s_ref[slot] = s
        smax_ref[slot] = jnp.broadcast_to(jnp.max(s, axis=0, keepdims=True), smax_ref.shape[1:])

    def softmax_step(slot, cols, s, s_max, j):
        m_prev = m_ref[slot, :, cols]
        m_new = jnp.maximum(m_prev, s_max)
        m_safe = jnp.where(m_new == -jnp.inf, 0.0, m_new)
        p = jnp.exp2(s - m_safe[0:1, :])
        alpha = jnp.exp2(m_prev - m_safe)
        acc_ref[slot, :, cols] = alpha[0:1, :] * acc_ref[slot, :, cols] + _dot(vt_ref[j], p.astype(BF16))
        m_ref[slot, :, cols] = m_new

    def update(slot, j):
        softmax_step(slot, slice(0, 2 * tq), s_ref[slot], smax_ref[slot], j)

    def diag_scores(slot, j, d):
        w = tq - d * tk
        rhs = q2t if d == 0 else jnp.concatenate([q2t[:, d * tk:tq], q2t[:, tq + d * tk:]], axis=1)
        s_ref[slot, :, 0:2 * w] = _dot(key_block(j), rhs)

    def diag_update(slot, j, d):
        w = tq - d * tk
        key = lax.broadcasted_iota(jnp.int32, (tk, tk), 0)
        qry = lax.broadcasted_iota(jnp.int32, (tk, tk), 1)
        for hd in range(2):
            s = s_ref[slot, :, hd * w:(hd + 1) * w]
            tri = jnp.where(qry >= key, s[:, 0:tk], -jnp.inf)
            s = tri if w == tk else jnp.concatenate([tri, s[:, tk:]], axis=1)
            softmax_step(slot, slice(hd * tq + d * tk, (hd + 1) * tq), s,
                         jnp.max(s, axis=0, keepdims=True), j)

    scores(0, 0)

    def pair(i):
        scores(1, 2 * i + 1)
        update(0, 2 * i)
        scores(0, 2 * i + 2)
        update(1, 2 * i + 1)

    def two_pairs(i, carry):
        pair(2 * i)
        pair(2 * i + 1)
        return carry

    assert KEY_BLOCKS_PER_QUERY_BLOCK == 4 and tq == 4 * tk
    lax.fori_loop(0, qi, two_pairs, 0)
    j0 = KEY_BLOCKS_PER_QUERY_BLOCK * qi
    diag_scores(1, j0 + 1, 1)
    diag_update(0, j0, 0)
    diag_scores(0, j0 + 2, 2)
    diag_update(1, j0 + 1, 1)
    diag_scores(1, j0 + 3, 3)
    diag_update(0, j0 + 2, 2)
    diag_update(1, j0 + 3, 3)

    m = jnp.maximum(m_ref[0], m_ref[1])
    w0 = jnp.exp2(m_ref[0] - m)
    w1 = jnp.exp2(m_ref[1] - m)
    acc = w0[0:1, :] * acc_ref[0] + w1[0:1, :] * acc_ref[1]
    o_t = jnp.concatenate([acc[0:HEAD_DIM, 0:tq] / acc[LANES:LANES + 1, 0:tq],
                           acc[HEAD_DIM:LANES, tq:] / acc[LANES:LANES + 1, tq:]], axis=0)
    o_ref[...] = (o_t.T * _sigmoid(gate_ref[...].astype(F32))).astype(o_ref.dtype)


def _fox_attn(qt, kp, vt, proj, *, batch, seq_len, tq, tk, name):
    t = kp.shape[0]
    d = kp.shape[1] // 2
    n_pairs = d // LANES
    nq = seq_len // tq
    nk = seq_len // tk
    assert seq_len % tq == 0 and tq == KEY_BLOCKS_PER_QUERY_BLOCK * tk and tk % LANES == 0
    assert qt.shape == (batch, n_pairs, nq, 2 * LANES, tq) and vt.shape == (batch, n_pairs, nk, VT_ROWS, tk)
    g_blk = 3 * d // LANES
    return pl.pallas_call(
        functools.partial(_fox_attn_kernel, tq=tq, tk=tk),
        grid=(batch, n_pairs, nq),
        in_specs=[
            pl.BlockSpec((None, None, None, 2 * LANES, tq), lambda b, p, i: (b, p, i, 0, 0)),
            pl.BlockSpec((seq_len, 2 * LANES), lambda b, p, i: (b, p)),
            pl.BlockSpec((None, None, nk, VT_ROWS, tk), lambda b, p, i: (b, p, 0, 0, 0)),
            pl.BlockSpec((tq, LANES), lambda b, p, i: (b * nq + i, g_blk + p)),
        ],
        out_specs=pl.BlockSpec((tq, LANES), lambda b, p, i: (b * nq + i, p)),
        out_shape=jax.ShapeDtypeStruct((t, d), BF16),
        scratch_shapes=[
            pltpu.VMEM((2, SUBLANES, 2 * tq), F32),
            pltpu.VMEM((2, VT_ROWS, 2 * tq), F32),
            pltpu.VMEM((2, tk, 2 * tq), F32),
            pltpu.VMEM((2, SUBLANES, 2 * tq), F32),
        ],
        compiler_params=_cparams("parallel", "parallel", "arbitrary"),
        name=name,
    )(qt, kp, vt, proj)


def _tile(n, pref):
    tile = min(n, pref)
    while n % tile:
        tile //= 2
    return tile


def kernel(x, mix_norm_g, ffn_norm_g, ssd_w_in, ssd_conv_w, ssd_conv_b, ssd_dt_bias, ssd_a_log, ssd_d,
           ssd_norm_g, ssd_w_out, fox_w_in, fox_b_f, fox_q_norm_g, fox_k_norm_g, fox_w_out,
           ffn_w_up, ffn_conv_w, ffn_conv_b, ffn_w_down, final_norm_g):
    batch, seq_len, d_model = x.shape
    t = batch * seq_len
    depth = mix_norm_g.shape[0]
    tm_proj = _tile(seq_len, 512)
    tm_ffn = _tile(seq_len, 1024)
    tm_out = _tile(seq_len, 1024)
    ts_ssd = _tile(seq_len, 512)
    tq = _tile(seq_len, 1024)
    ts_prep = tq

    def lane_pad(w):
        return jnp.pad(w, ((0, 0), (0, LANES - w.shape[1])))

    ssd_w_in_b, ssd_w_out_b = ssd_w_in.astype(BF16), ssd_w_out.astype(BF16)
    fox_w_in_b, fox_w_out_b = fox_w_in.astype(BF16), fox_w_out.astype(BF16)
    ffn_w_up_b, ffn_w_down_b = ffn_w_up.astype(BF16), ffn_w_down.astype(BF16)
    tk = tq // KEY_BLOCKS_PER_QUERY_BLOCK

    h = x.reshape(t, d_model)
    for i in range(depth):
        j = i // 2
        if i % 2 == 0:
            n_heads = ssd_d.shape[1]
            n_main = ssd_w_in.shape[2] - n_heads
            proj, dt_raw = _ssd_in_proj(
                h, mix_norm_g[i], ssd_w_in_b, j, n_main, lane_pad(ssd_w_in_b[j][:, n_main:]),
                ssd_conv_w[j], ssd_conv_b[j], d_inner=n_heads * HEAD_DIM,
                tm=tm_proj, chunk=512, seq_len=seq_len, name=f"ssd_in_proj_{j}")
            y = _ssd_core(proj, dt_raw, ssd_dt_bias[j], ssd_a_log[j], ssd_d[j], ssd_norm_g[j],
                          batch=batch, seq_len=seq_len, ts=ts_ssd, name=f"ssd_core_{j}")
            h = _matmul_residual(y, ssd_w_out_b, j, h, tm=tm_out, name=f"ssd_out_proj_{j}", in_place=i > 0)
        else:
            n_heads = fox_b_f.shape[1]
            n_main = fox_w_in.shape[2] - n_heads
            proj, flog = _norm_matmul(
                h, mix_norm_g[i], fox_w_in_b, j, n_main, lane_pad(fox_w_in_b[j][:, n_main:]),
                tm=tm_proj, chunk=512, name=f"fox_in_proj_{j}")
            qt, kp, vt = _fox_prep(proj, flog, fox_b_f[j], fox_q_norm_g[j], fox_k_norm_g[j],
                                   batch=batch, seq_len=seq_len, ts=ts_prep, tq=tq, tk=tk, name=f"fox_prep_{j}")
            o = _fox_attn(qt, kp, vt, proj, batch=batch, seq_len=seq_len, tq=tq, tk=tk, name=f"fox_attn_{j}")
            h = _matmul_residual(o, fox_w_out_b, j, h, tm=tm_out, name=f"fox_out_proj_{j}")
        act = _ffn_up(h, ffn_norm_g[i], ffn_w_up_b, i, ffn_conv_w[i], ffn_conv_b[i],
                      tm=tm_ffn, chunk=256, seq_len=seq_len, name=f"ffn_up_{i}")
        h = _matmul_residual(act, ffn_w_down_b, i, h, final_norm_g if i == depth - 1 else None,
                             tm=tm_out, name=f"ffn_down_{i}")
    return h.reshape(batch, seq_len, d_model)
```

```python
import functools

import numpy as np
import jax
import jax.numpy as jnp
from jax import lax
from jax.experimental import pallas as pl
from jax.experimental.pallas import tpu as pltpu

F32 = jnp.float32
BF16 = jnp.bfloat16

EPS = 1e-6
LOG2_E = 1.4426950408889634
LANES = 128
SUBLANES = 8
HEAD_DIM = 64
SSD_CHUNK = 128
SSD_GROUPS = 4
SSD_STATE = 128
SSD_CONV = 4
FFN_CONV = 3
VMEM_LIMIT_BYTES = 56 * 1024 * 1024


def _cparams(*sem):
    return pltpu.CompilerParams(dimension_semantics=sem, vmem_limit_bytes=VMEM_LIMIT_BYTES)


def _sigmoid(x):
    return 1.0 / (1.0 + jnp.exp(-x))


def _softplus(x):
    return jnp.maximum(x, 0.0) + jnp.log1p(jnp.exp(-jnp.abs(x)))


def _split_bf16(v, parts):
    out = []
    for _ in range(parts - 1):
        p = v.astype(BF16)
        out.append(p)
        v = v - p.astype(F32)
    out.append(v.astype(BF16))
    return out


def _dot(a, b):
    return jnp.dot(a, b, preferred_element_type=F32)


def _dot_nt(a, b):
    return lax.dot_general(a, b, (((1,), (1,)), ((), ())), preferred_element_type=F32)


def _select_dot(sel, v, parts):
    acc = None
    for p in _split_bf16(v, parts):
        t = _dot(sel, p)
        acc = t if acc is None else acc + t
    return acc


def _rms_normed(x, g):
    ms = jnp.mean(x * x, axis=-1, keepdims=True)
    return (x * lax.rsqrt(ms + EPS) * g).astype(BF16)


def _resident(shape, layer=None):
    if layer is None:
        return pl.BlockSpec(shape, lambda *_: (0,) * len(shape), pipeline_mode=pl.Buffered(1))
    return pl.BlockSpec((None,) + tuple(shape), lambda *_: (layer,) + (0,) * len(shape),
                        pipeline_mode=pl.Buffered(1))


def _norm_matmul_kernel(x_ref, g_ref, w_ref, wa_ref, o_ref, oa_ref, *, chunk):
    xn = _rms_normed(x_ref[...], g_ref[...])
    oa_ref[...] = _dot(xn, wa_ref[...])
    for c in range(w_ref.shape[1] // chunk):
        sl = slice(c * chunk, (c + 1) * chunk)
        o_ref[:, sl] = _dot(xn, w_ref[:, sl]).astype(o_ref.dtype)


def _norm_matmul(x, g, w_stack, layer, n, w_aux, *, tm, chunk, name):
    t, d = x.shape
    assert t % tm == 0 and n % chunk == 0 and n % LANES == 0
    return pl.pallas_call(
        functools.partial(_norm_matmul_kernel, chunk=chunk),
        grid=(t // tm,),
        in_specs=[
            pl.BlockSpec((tm, d), lambda i: (i, 0)),
            _resident((1, d)),
            _resident((d, n), layer),
            _resident((d, LANES)),
        ],
        out_specs=[pl.BlockSpec((tm, n), lambda i: (i, 0)), pl.BlockSpec((tm, LANES), lambda i: (i, 0))],
        out_shape=[jax.ShapeDtypeStruct((t, n), BF16), jax.ShapeDtypeStruct((t, LANES), F32)],
        compiler_params=_cparams("parallel"),
        name=name,
    )(x, g.reshape(1, d), w_stack, w_aux)


def _matmul_residual_kernel(*refs, final):
    if final:
        y_ref, w_ref, h_ref, fg_ref, o_ref = refs
    else:
        y_ref, w_ref, h_ref, o_ref = refs
    out = h_ref[...] + _dot(y_ref[...], w_ref[...])
    if final:
        ms = jnp.mean(out * out, axis=-1, keepdims=True)
        out = out * lax.rsqrt(ms + EPS) * fg_ref[...]
    o_ref[...] = out


def _matmul_residual(y, w_stack, layer, h, final_g=None, *, tm, name, in_place=True):
    t, k = y.shape
    d = w_stack.shape[2]
    assert t % tm == 0 and w_stack.shape[1] == k
    final = final_g is not None
    in_specs = [
        pl.BlockSpec((tm, k), lambda i: (i, 0)),
        _resident((k, d), layer),
        pl.BlockSpec((tm, d), lambda i: (i, 0)),
    ]
    args = [y, w_stack, h]
    if final:
        in_specs.append(_resident((1, d)))
        args.append(final_g.reshape(1, d))
    return pl.pallas_call(
        functools.partial(_matmul_residual_kernel, final=final),
        grid=(t // tm,),
        in_specs=in_specs,
        out_specs=pl.BlockSpec((tm, d), lambda i: (i, 0)),
        out_shape=jax.ShapeDtypeStruct((t, d), F32),
        input_output_aliases={2: 0} if in_place else {},
        compiler_params=_cparams("parallel"),
        name=name,
    )(*args)


FFN_HALO = 16


def _ffn_up_kernel(x_ref, xh_ref, g_ref, w_ref, cw_ref, cb_ref, o_ref, xn_ref, buf_ref, *, tm, f, chunk, seq_len):
    seq_start = (pl.program_id(0) * tm) % seq_len == 0
    xn_ref[0:tm, :] = _rms_normed(x_ref[...], g_ref[...])
    xn_ref[tm:, :] = _rms_normed(xh_ref[...], g_ref[...])
    for c in range(f // chunk):
        sl = slice(c * chunk, (c + 1) * chunk)
        buf = buf_ref.at[c % 2]
        gate = _dot(xn_ref[...], w_ref[:, sl])
        up = _dot(xn_ref[0:tm, :], w_ref[:, f + c * chunk:f + (c + 1) * chunk])
        buf[0:SUBLANES, :] = jnp.where(seq_start, 0.0, gate[tm + FFN_HALO - SUBLANES:, :])
        buf[SUBLANES:, :] = gate[0:tm, :]
        conv = cb_ref[:, sl]
        for k in range(FFN_CONV):
            off = SUBLANES - (FFN_CONV - 1) + k
            conv = conv + buf[off:off + tm, :] * cw_ref[k:k + 1, sl]
        sig = 1.0 / (1.0 + jnp.exp2(conv * (-LOG2_E)))
        o_ref[:, sl] = (conv * sig * up).astype(o_ref.dtype)


def _ffn_up(h, g, w_stack, layer, conv_w, conv_b, *, tm, chunk, seq_len, name):
    t, d = h.shape
    f = w_stack.shape[2] // 2
    assert t % tm == 0 and seq_len % tm == 0 and tm % FFN_HALO == 0 and f % chunk == 0
    return pl.pallas_call(
        functools.partial(_ffn_up_kernel, tm=tm, f=f, chunk=chunk, seq_len=seq_len),
        grid=(t // tm,),
        in_specs=[
            pl.BlockSpec((tm, d), lambda i: (i, 0)),
            pl.BlockSpec((FFN_HALO, d), lambda i: (jnp.maximum(i * (tm // FFN_HALO) - 1, 0), 0)),
            _resident((1, d)),
            _resident((d, 2 * f), layer),
            _resident((FFN_CONV, f)),
            _resident((1, f)),
        ],
        out_specs=pl.BlockSpec((tm, f), lambda i: (i, 0)),
        out_shape=jax.ShapeDtypeStruct((t, f), BF16),
        scratch_shapes=[
            pltpu.VMEM((tm + FFN_HALO, d), BF16),
            pltpu.VMEM((2, tm + SUBLANES, chunk), F32),
        ],
        compiler_params=_cparams("parallel"),
        name=name,
    )(h, h, g.reshape(1, d), w_stack, conv_w, conv_b.reshape(1, f))


def _ssd_in_proj_kernel(x_ref, xh_ref, g_ref, w_ref, wa_ref, cw_ref, cb_ref, o_ref, oa_ref, xn_ref, buf_ref,
                        *, tm, d_inner, chunk, seq_len):
    seq_start = (pl.program_id(0) * tm) % seq_len == 0
    xn_ref[0:tm, :] = _rms_normed(x_ref[...], g_ref[...])
    xn_ref[tm:, :] = _rms_normed(xh_ref[...], g_ref[...])
    oa_ref[...] = _dot(xn_ref[0:tm, :], wa_ref[...])
    for c in range(w_ref.shape[1] // chunk):
        sl = slice(c * chunk, (c + 1) * chunk)
        if c * chunk < d_inner:
            pre = _dot(xn_ref[0:tm, :], w_ref[:, sl])
        else:
            csl = slice(c * chunk - d_inner, (c + 1) * chunk - d_inner)
            buf = buf_ref.at[c % 2]
            ext = _dot(xn_ref[...], w_ref[:, sl])
            buf[0:SUBLANES, :] = jnp.where(seq_start, 0.0, ext[tm + FFN_HALO - SUBLANES:, :])
            buf[SUBLANES:, :] = ext[0:tm, :]
            pre = cb_ref[:, csl]
            for k in range(SSD_CONV):
                off = SUBLANES - (SSD_CONV - 1) + k
                pre = pre + buf[off:off + tm, :] * cw_ref[k:k + 1, csl]
        o_ref[:, sl] = (pre / (1.0 + jnp.exp2(pre * (-LOG2_E)))).astype(o_ref.dtype)


def _ssd_in_proj(h, g, w_stack, layer, n, w_aux, conv_w, conv_b, *, d_inner, tm, chunk, seq_len, name):
    t, d = h.shape
    n_conv = n - d_inner
    assert t % tm == 0 and seq_len % tm == 0 and tm % FFN_HALO == 0
    assert n % chunk == 0 and d_inner % chunk == 0 and conv_w.shape == (SSD_CONV, n_conv)
    return pl.pallas_call(
        functools.partial(_ssd_in_proj_kernel, tm=tm, d_inner=d_inner, chunk=chunk, seq_len=seq_len),
        grid=(t // tm,),
        in_specs=[
            pl.BlockSpec((tm, d), lambda i: (i, 0)),
            pl.BlockSpec((FFN_HALO, d), lambda i: (jnp.maximum(i * (tm // FFN_HALO) - 1, 0), 0)),
            _resident((1, d)),
            _resident((d, n), layer),
            _resident((d, LANES)),
            _resident((SSD_CONV, n_conv)),
            _resident((1, n_conv)),
        ],
        out_specs=[pl.BlockSpec((tm, n), lambda i: (i, 0)), pl.BlockSpec((tm, LANES), lambda i: (i, 0))],
        out_shape=[jax.ShapeDtypeStruct((t, n), BF16), jax.ShapeDtypeStruct((t, LANES), F32)],
        scratch_shapes=[
            pltpu.VMEM((tm + FFN_HALO, d), BF16),
            pltpu.VMEM((2, tm + SUBLANES, chunk), F32),
        ],
        compiler_params=_cparams("parallel"),
        name=name,
    )(h, h, g.reshape(1, d), w_stack, w_aux, conv_w, conv_b.reshape(1, n_conv))


def _ssd_kernel(z_ref, x_ref, bc_ref, dt_ref, dtb_ref, alog_ref, dsk_ref, ng_ref,
                e_ref, tri_ref, o_ref, state_ref, *, ts):
    d_inner = x_ref.shape[1]
    gw = d_inner // SSD_GROUPS
    gn = SSD_GROUPS * SSD_STATE

    @pl.when(pl.program_id(1) == 0)
    def _():
        state_ref[...] = jnp.zeros_like(state_ref)

    dt_all = _softplus(dt_ref[...] + dtb_ref[...])
    da_all = dt_all * (-jnp.exp(alog_ref[...]))

    ll = SSD_CHUNK
    row = lax.broadcasted_iota(jnp.int32, (ll, ll), 0)
    col = lax.broadcasted_iota(jnp.int32, (ll, ll), 1)
    causal = row >= col
    first_head = lax.broadcasted_iota(jnp.int32, (ll, LANES), 1) < HEAD_DIM
    expand = e_ref[...]

    for c in range(ts // ll):
        r0 = c * ll
        xs = x_ref[r0:r0 + ll, :].astype(F32)
        bc = bc_ref[r0:r0 + ll, :]
        dt = dt_all[r0:r0 + ll]
        acs = _select_dot(tri_ref[...], da_all[r0:r0 + ll], 3)
        acs_t = acs.T
        tot = acs[ll - 1:ll, :]
        tile16 = (2 * SUBLANES, LANES)
        per_head = jnp.concatenate(
            [jnp.exp(acs).astype(BF16), dt.astype(BF16), jnp.exp(tot - acs).astype(BF16)]
            + [jnp.broadcast_to(p, tile16) for p in _split_bf16(jnp.exp(tot), 3)], axis=0)
        per_chan = _dot(per_head, expand)
        ea_x = per_chan[0:ll]
        dt_x = per_chan[ll:2 * ll]
        ds_x = per_chan[2 * ll:3 * ll]
        r1 = 3 * ll
        chunk_decay_x = (per_chan[r1:r1 + 1] + per_chan[r1 + 16:r1 + 17]) + per_chan[r1 + 32:r1 + 33]
        xdt = xs * dt_x
        x_state = (xdt * ds_x).astype(BF16)

        ys = []
        for g in range(SSD_GROUPS):
            b_g = bc[:, g * SSD_STATE:(g + 1) * SSD_STATE]
            c_g = bc[:, gn + g * SSD_STATE:gn + (g + 1) * SSD_STATE]
            cb = _dot_nt(c_g, b_g)
            st = state_ref[g]
            y_off = _dot(c_g, st.astype(BF16)) * ea_x[:, g * gw:(g + 1) * gw]
            y_diag = []
            for pr in range(gw // LANES):
                h0 = (g * gw + pr * LANES) // HEAD_DIM
                ms = []
                for hh in (h0, h0 + 1):
                    diff = acs[:, hh:hh + 1] - acs_t[hh:hh + 1, :]
                    dec = jnp.exp(jnp.where(causal, diff, -jnp.inf))
                    ms.append((cb * dec).astype(BF16))
                lhs = jnp.concatenate(ms, axis=1)
                xp = xdt[:, g * gw + pr * LANES:g * gw + (pr + 1) * LANES]
                rhs = jnp.concatenate([jnp.where(first_head, xp, 0.0),
                                       jnp.where(first_head, 0.0, xp)], axis=0).astype(BF16)
                y_diag.append(_dot(lhs, rhs))
            ys.append(jnp.concatenate(y_diag, axis=1) + y_off)
            new = _dot(b_g.astype(F32).T.astype(BF16), x_state[:, g * gw:(g + 1) * gw])
            state_ref[g] = st * chunk_decay_x[:, g * gw:(g + 1) * gw] + new

        y = jnp.concatenate(ys, axis=1) + xs * dsk_ref[...]
        yz = y * z_ref[r0:r0 + ll, :].astype(F32)
        outs = []
        for g in range(SSD_GROUPS):
            blk = yz[:, g * gw:(g + 1) * gw]
            ms = jnp.mean(blk * blk, axis=-1, keepdims=True)
            outs.append(blk * lax.rsqrt(ms + EPS))
        o_ref[r0:r0 + ll, :] = (jnp.concatenate(outs, axis=1) * ng_ref[...]).astype(o_ref.dtype)


def _ssd_core(proj, dt_raw, dt_bias, a_log, d_skip, norm_g, *, batch, seq_len, ts, name):
    t = proj.shape[0]
    n_heads = d_skip.shape[0]
    d_inner = n_heads * HEAD_DIM
    gn = SSD_GROUPS * SSD_STATE
    assert proj.shape[1] == 2 * d_inner + 2 * gn and (2 * d_inner) % (2 * gn) == 0
    assert seq_len % ts == 0 and ts % SSD_CHUNK == 0 and n_heads <= LANES
    ns = seq_len // ts
    bc_blk = 2 * d_inner // (2 * gn)

    def pad_lanes(v):
        return jnp.pad(v.astype(F32), (0, LANES - v.shape[0])).reshape(1, LANES)

    expand = np.zeros((LANES, d_inner), np.float32)
    expand[np.arange(d_inner) // HEAD_DIM, np.arange(d_inner)] = 1.0
    tri = np.tril(np.ones((SSD_CHUNK, SSD_CHUNK), np.float32))

    def rows(b, s):
        return b * ns + s

    return pl.pallas_call(
        functools.partial(_ssd_kernel, ts=ts),
        grid=(batch, ns),
        in_specs=[
            pl.BlockSpec((ts, d_inner), lambda b, s: (rows(b, s), 0)),
            pl.BlockSpec((ts, d_inner), lambda b, s: (rows(b, s), 1)),
            pl.BlockSpec((ts, 2 * gn), lambda b, s: (rows(b, s), bc_blk)),
            pl.BlockSpec((ts, LANES), lambda b, s: (rows(b, s), 0)),
            _resident((1, LANES)),
            _resident((1, LANES)),
            _resident((1, d_inner)),
            _resident((1, d_inner)),
            _resident((LANES, d_inner)),
            _resident((SSD_CHUNK, SSD_CHUNK)),
        ],
        out_specs=pl.BlockSpec((ts, d_inner), lambda b, s: (rows(b, s), 0)),
        out_shape=jax.ShapeDtypeStruct((t, d_inner), BF16),
        scratch_shapes=[pltpu.VMEM((SSD_GROUPS, SSD_STATE, d_inner // SSD_GROUPS), F32)],
        compiler_params=_cparams("parallel", "arbitrary"),
        name=name,
    )(proj, proj, proj, dt_raw,
      pad_lanes(dt_bias), pad_lanes(a_log),
      jnp.repeat(d_skip, HEAD_DIM).reshape(1, d_inner), norm_g.reshape(1, d_inner),
      jnp.asarray(expand, BF16), jnp.asarray(tri, BF16))


GATE_PARTS = 3
GATE_LANES = 2 * GATE_PARTS
VT_ROWS = LANES + 16
KEY_BLOCKS_PER_QUERY_BLOCK = 4
CUM_BLOCK = 256


def _fox_prep_kernel(qk_ref, v_ref, fl_ref, bf_ref, qg_ref, kg_ref, hs_ref, tri_ref, pq_ref, pk_ref,
                     qt_ref, ok_ref, vt_ref, carry_ref, *, ts, tq, tk, d, scale):
    @pl.when(pl.program_id(1) == 0)
    def _():
        carry_ref[...] = jnp.zeros_like(carry_ref)

    for j in range(d // LANES):
        for kk in range(ts // tk):
            vj = v_ref[kk * tk:(kk + 1) * tk, j * LANES:(j + 1) * LANES]
            vt_ref[j, kk, 0:LANES, :] = vj.T
            vt_ref[j, kk, LANES:, :] = jnp.ones((VT_ROWS - LANES, tk), BF16)

    logit = fl_ref[...] + bf_ref[...]
    log_f = jnp.minimum(logit, 0.0) - jnp.log1p(jnp.exp(-jnp.abs(logit)))
    carry = carry_ref[...]
    cum_blocks = []
    for r in range(ts // CUM_BLOCK):
        c = _select_dot(tri_ref[...], log_f[r * CUM_BLOCK:(r + 1) * CUM_BLOCK], 3) + carry
        cum_blocks.append(c)
        carry = c[CUM_BLOCK - 1:CUM_BLOCK, :]
    carry_ref[...] = carry
    cum = jnp.concatenate(cum_blocks, axis=0)
    pieces = _split_bf16(cum * LOG2_E, GATE_PARTS) + [jnp.ones((ts, LANES), BF16)]
    feats = jnp.concatenate(pieces, axis=1)
    gq_t = _dot(feats, pq_ref[...]).astype(BF16).T
    gk = _dot(feats, pk_ref[...]).astype(BF16)

    def head_norm(base, jj, g_ref, mult):
        sl = slice(2 * jj * LANES, (2 * jj + 2) * LANES)
        v = qk_ref[:, base + 2 * jj * LANES:base + (2 * jj + 2) * LANES].astype(F32)
        ssq = _dot((v * v).astype(BF16), hs_ref[...])
        vn = v * lax.rsqrt(ssq * (1.0 / HEAD_DIM) + EPS) * (g_ref[:, sl] * mult)
        return vn.astype(BF16)

    for jj in range(d // (2 * LANES)):
        kn = head_norm(d, jj, kg_ref, 1.0)
        qn = head_norm(0, jj, qg_ref, scale)
        for half in range(2):
            j = 2 * jj + half
            hl = slice(half * LANES, (half + 1) * LANES)
            ok_ref[:, 2 * j * LANES:(2 * j + 1) * LANES] = kn[:, hl]
            ok_ref[:, (2 * j + 1) * LANES:(2 * j + 2) * LANES] = gk
            qn_t = qn[:, hl].astype(F32).T.astype(BF16)
            qt_ref[j, 0:LANES, :] = qn_t
            qt_ref[j, LANES:, :] = gq_t


def _gate_placements(n_heads):
    assert n_heads * GATE_LANES <= LANES
    pq = np.zeros((4 * LANES, LANES), np.float32)
    pk = np.zeros((4 * LANES, LANES), np.float32)
    ones_row = GATE_PARTS * LANES
    for head in range(n_heads):
        base = head * GATE_LANES
        for p in range(GATE_PARTS):
            pq[p * LANES + head, base + p] = 1.0
            pk[ones_row, base + p] = 1.0
            pq[ones_row, base + GATE_PARTS + p] = 1.0
            pk[p * LANES + head, base + GATE_PARTS + p] = -1.0
    return pq, pk


def _fox_prep(proj, flog, b_f, qg, kg, *, batch, seq_len, ts, tq, tk, name):
    t = proj.shape[0]
    n_heads = b_f.shape[0]
    d = n_heads * HEAD_DIM
    assert seq_len % ts == 0 and ts % tk == 0 and tq % ts == 0 and n_heads <= LANES
    assert ts % CUM_BLOCK == 0 and d % (2 * LANES) == 0
    ns = seq_len // ts
    n_pairs = d // LANES
    pq, pk = _gate_placements(n_heads)
    hs = np.kron(np.eye(2 * LANES // HEAD_DIM, dtype=np.float32), np.ones((HEAD_DIM, HEAD_DIM), np.float32))
    tri = np.tril(np.ones((CUM_BLOCK, CUM_BLOCK), np.float32))
    const = lambda b, s: (0, 0)
    rows = lambda b, s: (b * ns + s, 0)
    return pl.pallas_call(
        functools.partial(_fox_prep_kernel, ts=ts, tq=tq, tk=tk, d=d, scale=HEAD_DIM ** -0.5 * LOG2_E),
        grid=(batch, ns),
        in_specs=[
            pl.BlockSpec((ts, 2 * d), rows),
            pl.BlockSpec((ts, d), lambda b, s: (b * ns + s, 2)),
            pl.BlockSpec((ts, LANES), rows),
            pl.BlockSpec((1, LANES), const),
            pl.BlockSpec((1, d), const),
            pl.BlockSpec((1, d), const),
            pl.BlockSpec((2 * LANES, 2 * LANES), const),
            pl.BlockSpec((CUM_BLOCK, CUM_BLOCK), const),
            pl.BlockSpec((4 * LANES, LANES), const),
            pl.BlockSpec((4 * LANES, LANES), const),
        ],
        out_specs=[
            pl.BlockSpec((None, n_pairs, None, 2 * LANES, ts),
                         lambda b, s: (b, 0, s // (tq // ts), 0, s % (tq // ts))),
            pl.BlockSpec((ts, 2 * d), rows),
            pl.BlockSpec((None, n_pairs, ts // tk, VT_ROWS, tk), lambda b, s: (b, 0, s, 0, 0)),
        ],
        out_shape=[
            jax.ShapeDtypeStruct((batch, n_pairs, seq_len // tq, 2 * LANES, tq), BF16),
            jax.ShapeDtypeStruct((t, 2 * d), BF16),
            jax.ShapeDtypeStruct((batch, n_pairs, seq_len // tk, VT_ROWS, tk), BF16),
        ],
        scratch_shapes=[pltpu.VMEM((1, LANES), F32)],
        compiler_params=_cparams("parallel", "arbitrary"),
        name=name,
    )(proj, proj, flog,
      jnp.pad(b_f.astype(F32), (0, LANES - n_heads)).reshape(1, LANES),
      jnp.tile(qg, n_heads).reshape(1, d), jnp.tile(kg, n_heads).reshape(1, d),
      jnp.asarray(hs, BF16), jnp.asarray(tri, BF16), jnp.asarray(pq, BF16), jnp.asarray(pk, BF16))


def _fox_attn_kernel(qt_ref, k_ref, vt_ref, gate_ref, o_ref, m_ref, acc_ref, s_ref, smax_ref, *, tq, tk):
    qi = pl.program_id(2)
    lane = lax.broadcasted_iota(jnp.int32, (2 * LANES, 1), 0)
    g0 = LANES + 2 * GATE_LANES * pl.program_id(1)
    head_lanes = (
        (lane < HEAD_DIM) | ((lane >= g0) & (lane < g0 + GATE_LANES)),
        ((lane >= HEAD_DIM) & (lane < LANES)) | ((lane >= g0 + GATE_LANES) & (lane < g0 + 2 * GATE_LANES)),
    )
    qt = qt_ref[...]
    q2t = jnp.concatenate([jnp.where(sel, qt, jnp.zeros_like(qt)) for sel in head_lanes], axis=1)

    def key_block(j):
        return k_ref[pl.ds(pl.multiple_of(j * tk, tk), tk), :]

    def scores(slot, j):
        s = _dot(key_block(j), q2t)
        s_ref[slot] = s
        smax_ref[slot] = jnp.broadcast_to(jnp.max(s, axis=0, keepdims=True), smax_ref.shape[1:])

    def softmax_step(slot, cols, s, s_max, j):
        m_prev = m_ref[slot, :, cols]
        m_new = jnp.maximum(m_prev, s_max)
        m_safe = jnp.where(m_new == -jnp.inf, 0.0, m_new)
        p = jnp.exp2(s - m_safe[0:1, :])
        alpha = jnp.exp2(m_prev - m_safe)
        acc_ref[slot, :, cols] = alpha[0:1, :] * acc_ref[slot, :, cols] + _dot(vt_ref[j], p.astype(BF16))
        m_ref[slot, :, cols] = m_new

    def update(slot, j):
        softmax_step(slot, slice(0, 2 * tq), s_ref[slot], smax_ref[slot], j)

    def diag_scores(slot, j, d):
        w = tq - d * tk
        rhs = q2t if d == 0 else jnp.concatenate([q2t[:, d * tk:tq], q2t[:, tq + d * tk:]], axis=1)
        s_ref[slot, :, 0:2 * w] = _dot(key_block(j), rhs)

    def diag_update(slot, j, d):
        w = tq - d * tk
        key = lax.broadcasted_iota(jnp.int32, (tk, tk), 0)
        qry = lax.broadcasted_iota(jnp.int32, (tk, tk), 1)
        for hd in range(2):
            s = s_ref[slot, :, hd * w:(hd + 1) * w]
            tri = jnp.where(qry >= key, s[:, 0:tk], -jnp.inf)
            s = tri if w == tk else jnp.concatenate([tri, s[:, tk:]], axis=1)
            softmax_step(slot, slice(hd * tq + d * tk, (hd + 1) * tq), s,
                         jnp.max(s, axis=0, keepdims=True), j)

    scores(0, 0)
    m_ref[...] = jnp.full_like(m_ref, -jnp.inf)
    acc_ref[...] = jnp.zeros_like(acc_ref)

    def pair(i):
        scores(1, 2 * i + 1)
        update(0, 2 * i)
        scores(0, 2 * i + 2)
        update(1, 2 * i + 1)

    def two_pairs(i, carry):
        pair(2 * i)
        pair(2 * i + 1)
        return carry

    nkq = KEY_BLOCKS_PER_QUERY_BLOCK
    assert nkq % 4 == 0 and tq == nkq * tk
    lax.fori_loop(0, (nkq // 4) * qi, two_pairs, 0)
    j0 = nkq * qi
    for d in range(nkq):
        if d + 1 < nkq:
            diag_scores((d + 1) % 2, j0 + d + 1, d + 1)
        diag_update(d % 2, j0 + d, d)

    m = jnp.maximum(m_ref[0], m_ref[1])
    w0 = jnp.exp2(m_ref[0] - m)
    w1 = jnp.exp2(m_ref[1] - m)
    acc = w0[0:1, :] * acc_ref[0] + w1[0:1, :] * acc_ref[1]
    o_t = jnp.concatenate([acc[0:HEAD_DIM, 0:tq] / acc[LANES:LANES + 1, 0:tq],
                           acc[HEAD_DIM:LANES, tq:] / acc[LANES:LANES + 1, tq:]], axis=0)
    o_ref[...] = (o_t.T * _sigmoid(gate_ref[...].astype(F32))).astype(o_ref.dtype)


def _fox_attn(qt, kp, vt, proj, *, batch, seq_len, tq, tk, name):
    t = kp.shape[0]
    d = kp.shape[1] // 2
    n_pairs = d // LANES
    nq = seq_len // tq
    nk = seq_len // tk
    assert seq_len % tq == 0 and tq == KEY_BLOCKS_PER_QUERY_BLOCK * tk and tk % LANES == 0
    assert qt.shape == (batch, n_pairs, nq, 2 * LANES, tq) and vt.shape == (batch, n_pairs, nk, VT_ROWS, tk)
    g_blk = 3 * d // LANES
    return pl.pallas_call(
        functools.partial(_fox_attn_kernel, tq=tq, tk=tk),
        grid=(batch, n_pairs, nq),
        in_specs=[
            pl.BlockSpec((None, None, None, 2 * LANES, tq), lambda b, p, i: (b, p, i, 0, 0)),
            pl.BlockSpec((seq_len, 2 * LANES), lambda b, p, i: (b, p)),
            pl.BlockSpec((None, None, nk, VT_ROWS, tk), lambda b, p, i: (b, p, 0, 0, 0)),
            pl.BlockSpec((tq, LANES), lambda b, p, i: (b * nq + i, g_blk + p)),
        ],
        out_specs=pl.BlockSpec((tq, LANES), lambda b, p, i: (b * nq + i, p)),
        out_shape=jax.ShapeDtypeStruct((t, d), BF16),
        scratch_shapes=[
            pltpu.VMEM((2, SUBLANES, 2 * tq), F32),
            pltpu.VMEM((2, VT_ROWS, 2 * tq), F32),
            pltpu.VMEM((2, tk, 2 * tq), F32),
            pltpu.VMEM((2, SUBLANES, 2 * tq), F32),
        ],
        compiler_params=_cparams("parallel", "parallel", "arbitrary"),
        name=name,
    )(qt, kp, vt, proj)


def _tile(n, pref):
    tile = min(n, pref)
    while n % tile:
        tile //= 2
    return tile


def kernel(x, mix_norm_g, ffn_norm_g, ssd_w_in, ssd_conv_w, ssd_conv_b, ssd_dt_bias, ssd_a_log, ssd_d,
           ssd_norm_g, ssd_w_out, fox_w_in, fox_b_f, fox_q_norm_g, fox_k_norm_g, fox_w_out,
           ffn_w_up, ffn_conv_w, ffn_conv_b, ffn_w_down, final_norm_g):
    batch, seq_len, d_model = x.shape
    t = batch * seq_len
    depth = mix_norm_g.shape[0]
    tm_proj = _tile(seq_len, 512)
    tm_ffn = _tile(seq_len, 1024)
    tm_out = _tile(seq_len, 1024)
    ts_ssd = _tile(seq_len, 512)
    tq = _tile(seq_len, 1024)
    ts_prep = tq

    def lane_pad(w):
        return jnp.pad(w, ((0, 0), (0, LANES - w.shape[1])))

    n_ssd_main = ssd_w_in.shape[2] - ssd_d.shape[1]
    n_fox_main = fox_w_in.shape[2] - fox_b_f.shape[1]
    ssd_w_in_b, ssd_w_out_b = ssd_w_in[:, :, :n_ssd_main].astype(BF16), ssd_w_out.astype(BF16)
    fox_w_in_b, fox_w_out_b = fox_w_in[:, :, :n_fox_main].astype(BF16), fox_w_out.astype(BF16)
    ffn_w_up_b, ffn_w_down_b = ffn_w_up.astype(BF16), ffn_w_down.astype(BF16)
    tk = tq // KEY_BLOCKS_PER_QUERY_BLOCK

    h = x.reshape(t, d_model)
    for i in range(depth):
        j = i // 2
        if i % 2 == 0:
            proj, dt_raw = _ssd_in_proj(
                h, mix_norm_g[i], ssd_w_in_b, j, n_ssd_main,
                lane_pad(ssd_w_in[j][:, n_ssd_main:]).astype(BF16),
                ssd_conv_w[j], ssd_conv_b[j], d_inner=ssd_d.shape[1] * HEAD_DIM,
                tm=tm_proj, chunk=512, seq_len=seq_len, name=f"ssd_in_proj_{j}")
            y = _ssd_core(proj, dt_raw, ssd_dt_bias[j], ssd_a_log[j], ssd_d[j], ssd_norm_g[j],
                          batch=batch, seq_len=seq_len, ts=ts_ssd, name=f"ssd_core_{j}")
            h = _matmul_residual(y, ssd_w_out_b, j, h, tm=tm_out, name=f"ssd_out_proj_{j}", in_place=i > 0)
        else:
            proj, flog = _norm_matmul(
                h, mix_norm_g[i], fox_w_in_b, j, n_fox_main,
                lane_pad(fox_w_in[j][:, n_fox_main:]).astype(BF16),
                tm=tm_proj, chunk=512, name=f"fox_in_proj_{j}")
            qt, kp, vt = _fox_prep(proj, flog, fox_b_f[j], fox_q_norm_g[j], fox_k_norm_g[j],
                                   batch=batch, seq_len=seq_len, ts=ts_prep, tq=tq, tk=tk, name=f"fox_prep_{j}")
            o = _fox_attn(qt, kp, vt, proj, batch=batch, seq_len=seq_len, tq=tq, tk=tk, name=f"fox_attn_{j}")
            h = _matmul_residual(o, fox_w_out_b, j, h, tm=tm_out, name=f"fox_out_proj_{j}")
        act = _ffn_up(h, ffn_norm_g[i], ffn_w_up_b, i, ffn_conv_w[i], ffn_conv_b[i],
                      tm=tm_ffn, chunk=256, seq_len=seq_len, name=f"ffn_up_{i}")
        h = _matmul_residual(act, ffn_w_down_b, i, h, final_norm_g if i == depth - 1 else None,
                             tm=tm_out, name=f"ffn_down_{i}")
    return h.reshape(batch, seq_len, d_model)
```

```python
import functools

import numpy as np
import jax
import jax.numpy as jnp
from jax import lax
from jax.experimental import pallas as pl
from jax.experimental.pallas import tpu as pltpu

F32 = jnp.float32
BF16 = jnp.bfloat16

EPS = 1e-6
LOG2_E = 1.4426950408889634
LANES = 128
SUBLANES = 8
HEAD_DIM = 64
SSD_CHUNK = 128
SSD_GROUPS = 4
SSD_STATE = 128
SSD_CONV = 4
FFN_CONV = 3
VMEM_LIMIT_BYTES = 56 * 1024 * 1024


def _cparams(*sem):
    return pltpu.CompilerParams(dimension_semantics=sem, vmem_limit_bytes=VMEM_LIMIT_BYTES)


def _sigmoid(x):
    return 1.0 / (1.0 + jnp.exp(-x))


def _softplus(x):
    return jnp.maximum(x, 0.0) + jnp.log1p(jnp.exp(-jnp.abs(x)))


def _split_bf16(v, parts):
    out = []
    for _ in range(parts - 1):
        p = v.astype(BF16)
        out.append(p)
        v = v - p.astype(F32)
    out.append(v.astype(BF16))
    return out


def _dot(a, b):
    return jnp.dot(a, b, preferred_element_type=F32)


def _dot_nt(a, b):
    return lax.dot_general(a, b, (((1,), (1,)), ((), ())), preferred_element_type=F32)


def _select_dot(sel, v, parts):
    acc = None
    for p in _split_bf16(v, parts):
        t = _dot(sel, p)
        acc = t if acc is None else acc + t
    return acc


def _rms_normed(x, g):
    ms = jnp.mean(x * x, axis=-1, keepdims=True)
    return (x * lax.rsqrt(ms + EPS) * g).astype(BF16)


def _resident(shape, layer=None):
    if layer is None:
        return pl.BlockSpec(shape, lambda *_: (0,) * len(shape), pipeline_mode=pl.Buffered(1))
    return pl.BlockSpec((None,) + tuple(shape), lambda *_: (layer,) + (0,) * len(shape),
                        pipeline_mode=pl.Buffered(1))


def _norm_matmul_kernel(x_ref, g_ref, w_ref, wa_ref, o_ref, oa_ref, *, chunk):
    xn = _rms_normed(x_ref[...], g_ref[...])
    oa_ref[...] = _dot(xn, wa_ref[...])
    for c in range(w_ref.shape[1] // chunk):
        sl = slice(c * chunk, (c + 1) * chunk)
        o_ref[:, sl] = _dot(xn, w_ref[:, sl].astype(BF16)).astype(o_ref.dtype)


def _norm_matmul(x, g, w_stack, layer, n, w_aux, *, tm, chunk, name):
    t, d = x.shape
    assert t % tm == 0 and n % chunk == 0 and n % LANES == 0
    return pl.pallas_call(
        functools.partial(_norm_matmul_kernel, chunk=chunk),
        grid=(t // tm,),
        in_specs=[
            pl.BlockSpec((tm, d), lambda i: (i, 0)),
            _resident((1, d)),
            _resident((d, n), layer),
            _resident((d, LANES)),
        ],
        out_specs=[pl.BlockSpec((tm, n), lambda i: (i, 0)), pl.BlockSpec((tm, LANES), lambda i: (i, 0))],
        out_shape=[jax.ShapeDtypeStruct((t, n), BF16), jax.ShapeDtypeStruct((t, LANES), F32)],
        compiler_params=_cparams("parallel"),
        name=name,
    )(x, g.reshape(1, d), w_stack, w_aux)


def _matmul_residual_kernel(*refs, final):
    if final:
        y_ref, w_ref, h_ref, fg_ref, o_ref = refs
    else:
        y_ref, w_ref, h_ref, o_ref = refs
    out = h_ref[...] + _dot(y_ref[...], w_ref[...].astype(BF16))
    if final:
        ms = jnp.mean(out * out, axis=-1, keepdims=True)
        out = out * lax.rsqrt(ms + EPS) * fg_ref[...]
    o_ref[...] = out


def _matmul_residual(y, w_stack, layer, h, final_g=None, *, tm, name, in_place=True):
    t, k = y.shape
    d = w_stack.shape[2]
    assert t % tm == 0 and w_stack.shape[1] == k
    final = final_g is not None
    in_specs = [
        pl.BlockSpec((tm, k), lambda i: (i, 0)),
        _resident((k, d), layer),
        pl.BlockSpec((tm, d), lambda i: (i, 0)),
    ]
    args = [y, w_stack, h]
    if final:
        in_specs.append(_resident((1, d)))
        args.append(final_g.reshape(1, d))
    return pl.pallas_call(
        functools.partial(_matmul_residual_kernel, final=final),
        grid=(t // tm,),
        in_specs=in_specs,
        out_specs=pl.BlockSpec((tm, d), lambda i: (i, 0)),
        out_shape=jax.ShapeDtypeStruct((t, d), F32),
        input_output_aliases={2: 0} if in_place else {},
        compiler_params=_cparams("parallel"),
        name=name,
    )(*args)


FFN_HALO = 16


def _ffn_up_kernel(x_ref, xh_ref, g_ref, w_ref, cw_ref, cb_ref, o_ref, xn_ref, buf_ref, *, tm, f, chunk, seq_len):
    seq_start = (pl.program_id(0) * tm) % seq_len == 0
    xn_ref[0:tm, :] = _rms_normed(x_ref[...], g_ref[...])
    xn_ref[tm:, :] = _rms_normed(xh_ref[...], g_ref[...])
    for c in range(f // chunk):
        sl = slice(c * chunk, (c + 1) * chunk)
        buf = buf_ref.at[c % 2]
        gate = _dot(xn_ref[...], w_ref[:, sl].astype(BF16))
        up = _dot(xn_ref[0:tm, :], w_ref[:, f + c * chunk:f + (c + 1) * chunk].astype(BF16))
        buf[0:SUBLANES, :] = jnp.where(seq_start, 0.0, gate[tm + FFN_HALO - SUBLANES:, :])
        buf[SUBLANES:, :] = gate[0:tm, :]
        conv = cb_ref[:, sl]
        for k in range(FFN_CONV):
            off = SUBLANES - (FFN_CONV - 1) + k
            conv = conv + buf[off:off + tm, :] * cw_ref[k:k + 1, sl]
        sig = 1.0 / (1.0 + jnp.exp2(conv * (-LOG2_E)))
        o_ref[:, sl] = (conv * sig * up).astype(o_ref.dtype)


def _ffn_up(h, g, w_stack, layer, conv_w, conv_b, *, tm, chunk, seq_len, name):
    t, d = h.shape
    f = w_stack.shape[2] // 2
    assert t % tm == 0 and seq_len % tm == 0 and tm % FFN_HALO == 0 and f % chunk == 0
    return pl.pallas_call(
        functools.partial(_ffn_up_kernel, tm=tm, f=f, chunk=chunk, seq_len=seq_len),
        grid=(t // tm,),
        in_specs=[
            pl.BlockSpec((tm, d), lambda i: (i, 0)),
            pl.BlockSpec((FFN_HALO, d), lambda i: (jnp.maximum(i * (tm // FFN_HALO) - 1, 0), 0)),
            _resident((1, d)),
            _resident((d, 2 * f), layer),
            _resident((FFN_CONV, f)),
            _resident((1, f)),
        ],
        out_specs=pl.BlockSpec((tm, f), lambda i: (i, 0)),
        out_shape=jax.ShapeDtypeStruct((t, f), BF16),
        scratch_shapes=[
            pltpu.VMEM((tm + FFN_HALO, d), BF16),
            pltpu.VMEM((2, tm + SUBLANES, chunk), F32),
        ],
        compiler_params=_cparams("parallel"),
        name=name,
    )(h, h, g.reshape(1, d), w_stack, conv_w, conv_b.reshape(1, f))


def _ssd_in_proj_kernel(x_ref, xh_ref, g_ref, w_ref, wa_ref, cw_ref, cb_ref, o_ref, oa_ref, xn_ref, buf_ref,
                        *, tm, d_inner, chunk, seq_len):
    seq_start = (pl.program_id(0) * tm) % seq_len == 0
    xn_ref[0:tm, :] = _rms_normed(x_ref[...], g_ref[...])
    xn_ref[tm:, :] = _rms_normed(xh_ref[...], g_ref[...])
    oa_ref[...] = _dot(xn_ref[0:tm, :], wa_ref[...])
    for c in range(w_ref.shape[1] // chunk):
        sl = slice(c * chunk, (c + 1) * chunk)
        if c * chunk < d_inner:
            pre = _dot(xn_ref[0:tm, :], w_ref[:, sl])
        else:
            csl = slice(c * chunk - d_inner, (c + 1) * chunk - d_inner)
            buf = buf_ref.at[c % 2]
            ext = _dot(xn_ref[...], w_ref[:, sl])
            buf[0:SUBLANES, :] = jnp.where(seq_start, 0.0, ext[tm + FFN_HALO - SUBLANES:, :])
            buf[SUBLANES:, :] = ext[0:tm, :]
            pre = cb_ref[:, csl]
            for k in range(SSD_CONV):
                off = SUBLANES - (SSD_CONV - 1) + k
                pre = pre + buf[off:off + tm, :] * cw_ref[k:k + 1, csl]
        o_ref[:, sl] = (pre / (1.0 + jnp.exp2(pre * (-LOG2_E)))).astype(o_ref.dtype)


def _ssd_in_proj(h, g, w_stack, layer, n, w_aux, conv_w, conv_b, *, d_inner, tm, chunk, seq_len, name):
    t, d = h.shape
    n_conv = n - d_inner
    assert t % tm == 0 and seq_len % tm == 0 and tm % FFN_HALO == 0
    assert n % chunk == 0 and d_inner % chunk == 0 and conv_w.shape == (SSD_CONV, n_conv)
    return pl.pallas_call(
        functools.partial(_ssd_in_proj_kernel, tm=tm, d_inner=d_inner, chunk=chunk, seq_len=seq_len),
        grid=(t // tm,),
        in_specs=[
            pl.BlockSpec((tm, d), lambda i: (i, 0)),
            pl.BlockSpec((FFN_HALO, d), lambda i: (jnp.maximum(i * (tm // FFN_HALO) - 1, 0), 0)),
            _resident((1, d)),
            _resident((d, n), layer),
            _resident((d, LANES)),
            _resident((SSD_CONV, n_conv)),
            _resident((1, n_conv)),
        ],
        out_specs=[pl.BlockSpec((tm, n), lambda i: (i, 0)), pl.BlockSpec((tm, LANES), lambda i: (i, 0))],
        out_shape=[jax.ShapeDtypeStruct((t, n), BF16), jax.ShapeDtypeStruct((t, LANES), F32)],
        scratch_shapes=[
            pltpu.VMEM((tm + FFN_HALO, d), BF16),
            pltpu.VMEM((2, tm + SUBLANES, chunk), F32),
        ],
        compiler_params=_cparams("parallel"),
        name=name,
    )(h, h, g.reshape(1, d), w_stack, w_aux, conv_w, conv_b.reshape(1, n_conv))


def _ssd_kernel(z_ref, x_ref, bc_ref, dt_ref, dtb_ref, alog_ref, dsk_ref, ng_ref,
                e_ref, tri_ref, o_ref, state_ref, *, ts):
    d_inner = x_ref.shape[1]
    gw = d_inner // SSD_GROUPS
    gn = SSD_GROUPS * SSD_STATE

    @pl.when(pl.program_id(1) == 0)
    def _():
        state_ref[...] = jnp.zeros_like(state_ref)

    dt_all = _softplus(dt_ref[...] + dtb_ref[...])
    da_all = dt_all * (-jnp.exp(alog_ref[...]))

    ll = SSD_CHUNK
    row = lax.broadcasted_iota(jnp.int32, (ll, ll), 0)
    col = lax.broadcasted_iota(jnp.int32, (ll, ll), 1)
    causal = row >= col
    first_head = lax.broadcasted_iota(jnp.int32, (ll, LANES), 1) < HEAD_DIM
    expand = e_ref[...]

    for c in range(ts // ll):
        r0 = c * ll
        xs = x_ref[r0:r0 + ll, :].astype(F32)
        bc = bc_ref[r0:r0 + ll, :]
        dt = dt_all[r0:r0 + ll]
        acs = _select_dot(tri_ref[...], da_all[r0:r0 + ll], 3)
        acs_t = acs.T
        tot = acs[ll - 1:ll, :]
        tile16 = (2 * SUBLANES, LANES)
        per_head = jnp.concatenate(
            [jnp.exp(acs).astype(BF16), dt.astype(BF16), jnp.exp(tot - acs).astype(BF16)]
            + [jnp.broadcast_to(p, tile16) for p in _split_bf16(jnp.exp(tot), 3)], axis=0)
        per_chan = _dot(per_head, expand)
        ea_x = per_chan[0:ll]
        dt_x = per_chan[ll:2 * ll]
        ds_x = per_chan[2 * ll:3 * ll]
        r1 = 3 * ll
        chunk_decay_x = (per_chan[r1:r1 + 1] + per_chan[r1 + 16:r1 + 17]) + per_chan[r1 + 32:r1 + 33]
        xdt = xs * dt_x
        x_state = (xdt * ds_x).astype(BF16)

        ys = []
        for g in range(SSD_GROUPS):
            b_g = bc[:, g * SSD_STATE:(g + 1) * SSD_STATE]
            c_g = bc[:, gn + g * SSD_STATE:gn + (g + 1) * SSD_STATE]
            cb = _dot_nt(c_g, b_g)
            st = state_ref[g]
            y_off = _dot(c_g, st.astype(BF16)) * ea_x[:, g * gw:(g + 1) * gw]
            y_diag = []
            for pr in range(gw // LANES):
                h0 = (g * gw + pr * LANES) // HEAD_DIM
                ms = []
                for hh in (h0, h0 + 1):
                    diff = acs[:, hh:hh + 1] - acs_t[hh:hh + 1, :]
                    dec = jnp.exp(jnp.where(causal, diff, -jnp.inf))
                    ms.append((cb * dec).astype(BF16))
                lhs = jnp.concatenate(ms, axis=1)
                xp = xdt[:, g * gw + pr * LANES:g * gw + (pr + 1) * LANES]
                rhs = jnp.concatenate([jnp.where(first_head, xp, 0.0),
                                       jnp.where(first_head, 0.0, xp)], axis=0).astype(BF16)
                y_diag.append(_dot(lhs, rhs))
            ys.append(jnp.concatenate(y_diag, axis=1) + y_off)
            new = _dot(b_g.astype(F32).T.astype(BF16), x_state[:, g * gw:(g + 1) * gw])
            state_ref[g] = st * chunk_decay_x[:, g * gw:(g + 1) * gw] + new

        y = jnp.concatenate(ys, axis=1) + xs * dsk_ref[...]
        yz = y * z_ref[r0:r0 + ll, :].astype(F32)
        outs = []
        for g in range(SSD_GROUPS):
            blk = yz[:, g * gw:(g + 1) * gw]
            ms = jnp.mean(blk * blk, axis=-1, keepdims=True)
            outs.append(blk * lax.rsqrt(ms + EPS))
        o_ref[r0:r0 + ll, :] = (jnp.concatenate(outs, axis=1) * ng_ref[...]).astype(o_ref.dtype)


def _ssd_core(proj, dt_raw, dt_bias, a_log, d_skip, norm_g, *, batch, seq_len, ts, name):
    t = proj.shape[0]
    n_heads = d_skip.shape[0]
    d_inner = n_heads * HEAD_DIM
    gn = SSD_GROUPS * SSD_STATE
    assert proj.shape[1] == 2 * d_inner + 2 * gn and (2 * d_inner) % (2 * gn) == 0
    assert seq_len % ts == 0 and ts % SSD_CHUNK == 0 and n_heads <= LANES
    ns = seq_len // ts
    bc_blk = 2 * d_inner // (2 * gn)

    def pad_lanes(v):
        return jnp.pad(v.astype(F32), (0, LANES - v.shape[0])).reshape(1, LANES)

    expand = np.zeros((LANES, d_inner), np.float32)
    expand[np.arange(d_inner) // HEAD_DIM, np.arange(d_inner)] = 1.0
    tri = np.tril(np.ones((SSD_CHUNK, SSD_CHUNK), np.float32))

    def rows(b, s):
        return b * ns + s

    return pl.pallas_call(
        functools.partial(_ssd_kernel, ts=ts),
        grid=(batch, ns),
        in_specs=[
            pl.BlockSpec((ts, d_inner), lambda b, s: (rows(b, s), 0)),
            pl.BlockSpec((ts, d_inner), lambda b, s: (rows(b, s), 1)),
            pl.BlockSpec((ts, 2 * gn), lambda b, s: (rows(b, s), bc_blk)),
            pl.BlockSpec((ts, LANES), lambda b, s: (rows(b, s), 0)),
            _resident((1, LANES)),
            _resident((1, LANES)),
            _resident((1, d_inner)),
            _resident((1, d_inner)),
            _resident((LANES, d_inner)),
            _resident((SSD_CHUNK, SSD_CHUNK)),
        ],
        out_specs=pl.BlockSpec((ts, d_inner), lambda b, s: (rows(b, s), 0)),
        out_shape=jax.ShapeDtypeStruct((t, d_inner), BF16),
        scratch_shapes=[pltpu.VMEM((SSD_GROUPS, SSD_STATE, d_inner // SSD_GROUPS), F32)],
        compiler_params=_cparams("parallel", "arbitrary"),
        name=name,
    )(proj, proj, proj, dt_raw,
      pad_lanes(dt_bias), pad_lanes(a_log),
      jnp.repeat(d_skip, HEAD_DIM).reshape(1, d_inner), norm_g.reshape(1, d_inner),
      jnp.asarray(expand, BF16), jnp.asarray(tri, BF16))


GATE_PARTS = 3
GATE_LANES = 2 * GATE_PARTS
VT_ROWS = LANES + 16
KEY_BLOCKS_PER_QUERY_BLOCK = 4
CUM_BLOCK = 256


def _fox_prep_kernel(qk_ref, v_ref, fl_ref, bf_ref, qg_ref, kg_ref, hs_ref, tri_ref, pq_ref, pk_ref,
                     qt_ref, ok_ref, vt_ref, carry_ref, *, ts, tq, tk, d, scale):
    @pl.when(pl.program_id(1) == 0)
    def _():
        carry_ref[...] = jnp.zeros_like(carry_ref)

    for j in range(d // LANES):
        for kk in range(ts // tk):
            vj = v_ref[kk * tk:(kk + 1) * tk, j * LANES:(j + 1) * LANES]
            vt_ref[j, kk, 0:LANES, :] = vj.T
            vt_ref[j, kk, LANES:, :] = jnp.ones((VT_ROWS - LANES, tk), BF16)

    logit = fl_ref[...] + bf_ref[...]
    log_f = jnp.minimum(logit, 0.0) - jnp.log1p(jnp.exp(-jnp.abs(logit)))
    carry = carry_ref[...]
    cum_blocks = []
    for r in range(ts // CUM_BLOCK):
        c = _select_dot(tri_ref[...], log_f[r * CUM_BLOCK:(r + 1) * CUM_BLOCK], 3) + carry
        cum_blocks.append(c)
        carry = c[CUM_BLOCK - 1:CUM_BLOCK, :]
    carry_ref[...] = carry
    cum = jnp.concatenate(cum_blocks, axis=0)
    pieces = _split_bf16(cum * LOG2_E, GATE_PARTS) + [jnp.ones((ts, LANES), BF16)]
    feats = jnp.concatenate(pieces, axis=1)
    gq_t = _dot(feats, pq_ref[...]).astype(BF16).T
    gk = _dot(feats, pk_ref[...]).astype(BF16)

    def head_norm(base, jj, g_ref, mult):
        sl = slice(2 * jj * LANES, (2 * jj + 2) * LANES)
        v = qk_ref[:, base + 2 * jj * LANES:base + (2 * jj + 2) * LANES].astype(F32)
        ssq = _dot((v * v).astype(BF16), hs_ref[...])
        vn = v * lax.rsqrt(ssq * (1.0 / HEAD_DIM) + EPS) * (g_ref[:, sl] * mult)
        return vn.astype(BF16)

    for jj in range(d // (2 * LANES)):
        kn = head_norm(d, jj, kg_ref, 1.0)
        qn = head_norm(0, jj, qg_ref, scale)
        for half in range(2):
            j = 2 * jj + half
            hl = slice(half * LANES, (half + 1) * LANES)
            ok_ref[:, 2 * j * LANES:(2 * j + 1) * LANES] = kn[:, hl]
            ok_ref[:, (2 * j + 1) * LANES:(2 * j + 2) * LANES] = gk
            qn_t = qn[:, hl].astype(F32).T.astype(BF16)
            qt_ref[j, 0:LANES, :] = qn_t
            qt_ref[j, LANES:, :] = gq_t


def _gate_placements(n_heads):
    assert n_heads * GATE_LANES <= LANES
    pq = np.zeros((4 * LANES, LANES), np.float32)
    pk = np.zeros((4 * LANES, LANES), np.float32)
    ones_row = GATE_PARTS * LANES
    for head in range(n_heads):
        base = head * GATE_LANES
        for p in range(GATE_PARTS):
            pq[p * LANES + head, base + p] = 1.0
            pk[ones_row, base + p] = 1.0
            pq[ones_row, base + GATE_PARTS + p] = 1.0
            pk[p * LANES + head, base + GATE_PARTS + p] = -1.0
    return pq, pk


def _fox_prep(proj, flog, b_f, qg, kg, *, batch, seq_len, ts, tq, tk, name):
    t = proj.shape[0]
    n_heads = b_f.shape[0]
    d = n_heads * HEAD_DIM
    assert seq_len % ts == 0 and ts % tk == 0 and tq % ts == 0 and n_heads <= LANES
    assert ts % CUM_BLOCK == 0 and d % (2 * LANES) == 0
    ns = seq_len // ts
    n_pairs = d // LANES
    pq, pk = _gate_placements(n_heads)
    hs = np.kron(np.eye(2 * LANES // HEAD_DIM, dtype=np.float32), np.ones((HEAD_DIM, HEAD_DIM), np.float32))
    tri = np.tril(np.ones((CUM_BLOCK, CUM_BLOCK), np.float32))
    const = lambda b, s: (0, 0)
    rows = lambda b, s: (b * ns + s, 0)
    return pl.pallas_call(
        functools.partial(_fox_prep_kernel, ts=ts, tq=tq, tk=tk, d=d, scale=HEAD_DIM ** -0.5 * LOG2_E),
        grid=(batch, ns),
        in_specs=[
            pl.BlockSpec((ts, 2 * d), rows),
            pl.BlockSpec((ts, d), lambda b, s: (b * ns + s, 2)),
            pl.BlockSpec((ts, LANES), rows),
            pl.BlockSpec((1, LANES), const),
            pl.BlockSpec((1, d), const),
            pl.BlockSpec((1, d), const),
            pl.BlockSpec((2 * LANES, 2 * LANES), const),
            pl.BlockSpec((CUM_BLOCK, CUM_BLOCK), const),
            pl.BlockSpec((4 * LANES, LANES), const),
            pl.BlockSpec((4 * LANES, LANES), const),
        ],
        out_specs=[
            pl.BlockSpec((None, n_pairs, None, 2 * LANES, ts),
                         lambda b, s: (b, 0, s // (tq // ts), 0, s % (tq // ts))),
            pl.BlockSpec((ts, 2 * d), rows),
            pl.BlockSpec((None, n_pairs, ts // tk, VT_ROWS, tk), lambda b, s: (b, 0, s, 0, 0)),
        ],
        out_shape=[
            jax.ShapeDtypeStruct((batch, n_pairs, seq_len // tq, 2 * LANES, tq), BF16),
            jax.ShapeDtypeStruct((t, 2 * d), BF16),
            jax.ShapeDtypeStruct((batch, n_pairs, seq_len // tk, VT_ROWS, tk), BF16),
        ],
        scratch_shapes=[pltpu.VMEM((1, LANES), F32)],
        compiler_params=_cparams("parallel", "arbitrary"),
        name=name,
    )(proj, proj, flog,
      jnp.pad(b_f.astype(F32), (0, LANES - n_heads)).reshape(1, LANES),
      jnp.tile(qg, n_heads).reshape(1, d), jnp.tile(kg, n_heads).reshape(1, d),
      jnp.asarray(hs, BF16), jnp.asarray(tri, BF16), jnp.asarray(pq, BF16), jnp.asarray(pk, BF16))


def _fox_attn_kernel(qt_ref, k_ref, vt_ref, gate_ref, o_ref, m_ref, acc_ref, s_ref, smax_ref, *, tq, tk):
    qi = pl.program_id(2)
    lane = lax.broadcasted_iota(jnp.int32, (2 * LANES, 1), 0)
    g0 = LANES + 2 * GATE_LANES * pl.program_id(1)
    head_lanes = (
        (lane < HEAD_DIM) | ((lane >= g0) & (lane < g0 + GATE_LANES)),
        ((lane >= HEAD_DIM) & (lane < LANES)) | ((lane >= g0 + GATE_LANES) & (lane < g0 + 2 * GATE_LANES)),
    )
    qt = qt_ref[...]
    q2t = jnp.concatenate([jnp.where(sel, qt, jnp.zeros_like(qt)) for sel in head_lanes], axis=1)

    def key_block(j):
        return k_ref[pl.ds(pl.multiple_of(j * tk, tk), tk), :]

    def scores(slot, j):
        s = _dot(key_block(j), q2t)
        s_ref[slot] = s
        smax_ref[slot] = jnp.broadcast_to(jnp.max(s, axis=0, keepdims=True), smax_ref.shape[1:])

    def softmax_step(slot, cols, s, s_max, j):
        m_prev = m_ref[slot, :, cols]
        m_new = jnp.maximum(m_prev, s_max)
        m_safe = jnp.where(m_new == -jnp.inf, 0.0, m_new)
        p = jnp.exp2(s - m_safe[0:1, :])
        alpha = jnp.exp2(m_prev - m_safe)
        acc_ref[slot, :, cols] = alpha[0:1, :] * acc_ref[slot, :, cols] + _dot(vt_ref[j], p.astype(BF16))
        m_ref[slot, :, cols] = m_new

    def update(slot, j):
        softmax_step(slot, slice(0, 2 * tq), s_ref[slot], smax_ref[slot], j)

    def diag_scores(slot, j, d):
        w = tq - d * tk
        rhs = q2t if d == 0 else jnp.concatenate([q2t[:, d * tk:tq], q2t[:, tq + d * tk:]], axis=1)
        s_ref[slot, :, 0:2 * w] = _dot(key_block(j), rhs)

    def diag_update(slot, j, d):
        w = tq - d * tk
        key = lax.broadcasted_iota(jnp.int32, (tk, tk), 0)
        qry = lax.broadcasted_iota(jnp.int32, (tk, tk), 1)
        for hd in range(2):
            s = s_ref[slot, :, hd * w:(hd + 1) * w]
            tri = jnp.where(qry >= key, s[:, 0:tk], -jnp.inf)
            s = tri if w == tk else jnp.concatenate([tri, s[:, tk:]], axis=1)
            softmax_step(slot, slice(hd * tq + d * tk, (hd + 1) * tq), s,
                         jnp.max(s, axis=0, keepdims=True), j)

    scores(0, 0)
    m_ref[...] = jnp.full_like(m_ref, -jnp.inf)
    acc_ref[...] = jnp.zeros_like(acc_ref)

    def pair(i):
        scores(1, 2 * i + 1)
        update(0, 2 * i)
        scores(0, 2 * i + 2)
        update(1, 2 * i + 1)

    def two_pairs(i, carry):
        pair(2 * i)
        pair(2 * i + 1)
        return carry

    nkq = KEY_BLOCKS_PER_QUERY_BLOCK
    assert nkq % 4 == 0 and tq == nkq * tk
    lax.fori_loop(0, (nkq // 4) * qi, two_pairs, 0)
    j0 = nkq * qi
    for d in range(nkq):
        if d + 1 < nkq:
            diag_scores((d + 1) % 2, j0 + d + 1, d + 1)
        diag_update(d % 2, j0 + d, d)

    m = jnp.maximum(m_ref[0], m_ref[1])
    w0 = jnp.exp2(m_ref[0] - m)
    w1 = jnp.exp2(m_ref[1] - m)
    acc = w0[0:1, :] * acc_ref[0] + w1[0:1, :] * acc_ref[1]
    o_t = jnp.concatenate([acc[0:HEAD_DIM, 0:tq] / acc[LANES:LANES + 1, 0:tq],
                           acc[HEAD_DIM:LANES, tq:] / acc[LANES:LANES + 1, tq:]], axis=0)
    o_ref[...] = (o_t.T * _sigmoid(gate_ref[...].astype(F32))).astype(o_ref.dtype)


def _fox_attn(qt, kp, vt, proj, *, batch, seq_len, tq, tk, name):
    t = kp.shape[0]
    d = kp.shape[1] // 2
    n_pairs = d // LANES
    nq = seq_len // tq
    nk = seq_len // tk
    assert seq_len % tq == 0 and tq == KEY_BLOCKS_PER_QUERY_BLOCK * tk and tk % LANES == 0
    assert qt.shape == (batch, n_pairs, nq, 2 * LANES, tq) and vt.shape == (batch, n_pairs, nk, VT_ROWS, tk)
    g_blk = 3 * d // LANES
    return pl.pallas_call(
        functools.partial(_fox_attn_kernel, tq=tq, tk=tk),
        grid=(batch, n_pairs, nq),
        in_specs=[
            pl.BlockSpec((None, None, None, 2 * LANES, tq), lambda b, p, i: (b, p, i, 0, 0)),
            pl.BlockSpec((seq_len, 2 * LANES), lambda b, p, i: (b, p)),
            pl.BlockSpec((None, None, nk, VT_ROWS, tk), lambda b, p, i: (b, p, 0, 0, 0)),
            pl.BlockSpec((tq, LANES), lambda b, p, i: (b * nq + i, g_blk + p)),
        ],
        out_specs=pl.BlockSpec((tq, LANES), lambda b, p, i: (b * nq + i, p)),
        out_shape=jax.ShapeDtypeStruct((t, d), BF16),
        scratch_shapes=[
            pltpu.VMEM((2, SUBLANES, 2 * tq), F32),
            pltpu.VMEM((2, VT_ROWS, 2 * tq), F32),
            pltpu.VMEM((2, tk, 2 * tq), F32),
            pltpu.VMEM((2, SUBLANES, 2 * tq), F32),
        ],
        compiler_params=_cparams("parallel", "parallel", "arbitrary"),
        name=name,
    )(qt, kp, vt, proj)


def _tile(n, pref):
    tile = min(n, pref)
    while n % tile:
        tile //= 2
    return tile


def kernel(x, mix_norm_g, ffn_norm_g, ssd_w_in, ssd_conv_w, ssd_conv_b, ssd_dt_bias, ssd_a_log, ssd_d,
           ssd_norm_g, ssd_w_out, fox_w_in, fox_b_f, fox_q_norm_g, fox_k_norm_g, fox_w_out,
           ffn_w_up, ffn_conv_w, ffn_conv_b, ffn_w_down, final_norm_g):
    batch, seq_len, d_model = x.shape
    t = batch * seq_len
    depth = mix_norm_g.shape[0]
    tm_proj = _tile(seq_len, 512)
    tm_ffn = _tile(seq_len, 1024)
    tm_out = _tile(seq_len, 1024)
    ts_ssd = _tile(seq_len, 512)
    tq = _tile(seq_len, 1024)
    ts_prep = tq

    def lane_pad(w):
        return jnp.pad(w, ((0, 0), (0, LANES - w.shape[1])))

    n_ssd_main = ssd_w_in.shape[2] - ssd_d.shape[1]
    n_fox_main = fox_w_in.shape[2] - fox_b_f.shape[1]
    ssd_w_in_b = ssd_w_in.astype(BF16)
    tk = tq // KEY_BLOCKS_PER_QUERY_BLOCK

    h = x.reshape(t, d_model)
    for i in range(depth):
        j = i // 2
        if i % 2 == 0:
            proj, dt_raw = _ssd_in_proj(
                h, mix_norm_g[i], ssd_w_in_b, j, n_ssd_main,
                lane_pad(ssd_w_in[j][:, n_ssd_main:]).astype(BF16),
                ssd_conv_w[j], ssd_conv_b[j], d_inner=ssd_d.shape[1] * HEAD_DIM,
                tm=tm_proj, chunk=512, seq_len=seq_len, name=f"ssd_in_proj_{j}")
            y = _ssd_core(proj, dt_raw, ssd_dt_bias[j], ssd_a_log[j], ssd_d[j], ssd_norm_g[j],
                          batch=batch, seq_len=seq_len, ts=ts_ssd, name=f"ssd_core_{j}")
            h = _matmul_residual(y, ssd_w_out, j, h, tm=tm_out, name=f"ssd_out_proj_{j}", in_place=i > 0)
        else:
            proj, flog = _norm_matmul(
                h, mix_norm_g[i], fox_w_in, j, n_fox_main,
                lane_pad(fox_w_in[j][:, n_fox_main:]).astype(BF16),
                tm=tm_proj, chunk=512, name=f"fox_in_proj_{j}")
            qt, kp, vt = _fox_prep(proj, flog, fox_b_f[j], fox_q_norm_g[j], fox_k_norm_g[j],
                                   batch=batch, seq_len=seq_len, ts=ts_prep, tq=tq, tk=tk, name=f"fox_prep_{j}")
            o = _fox_attn(qt, kp, vt, proj, batch=batch, seq_len=seq_len, tq=tq, tk=tk, name=f"fox_attn_{j}")
            h = _matmul_residual(o, fox_w_out, j, h, tm=tm_out, name=f"fox_out_proj_{j}")
        act = _ffn_up(h, ffn_norm_g[i], ffn_w_up, i, ffn_conv_w[i], ffn_conv_b[i],
                      tm=tm_ffn, chunk=256, seq_len=seq_len, name=f"ffn_up_{i}")
        h = _matmul_residual(act, ffn_w_down, i, h, final_norm_g if i == depth - 1 else None,
                             tm=tm_out, name=f"ffn_down_{i}")
    return h.reshape(batch, seq_len, d_model)
```

```python
import functools

import numpy as np
import jax
import jax.numpy as jnp
from jax import lax
from jax.experimental import pallas as pl
from jax.experimental.pallas import tpu as pltpu

F32 = jnp.float32
BF16 = jnp.bfloat16

EPS = 1e-6
LOG2_E = 1.4426950408889634
LANES = 128
SUBLANES = 8
HEAD_DIM = 64
SSD_CHUNK = 128
SSD_GROUPS = 4
SSD_STATE = 128
SSD_CONV = 4
FFN_CONV = 3
VMEM_LIMIT_BYTES = 56 * 1024 * 1024


def _cparams(*sem):
    return pltpu.CompilerParams(dimension_semantics=sem, vmem_limit_bytes=VMEM_LIMIT_BYTES)


def _sigmoid(x):
    return 1.0 / (1.0 + jnp.exp(-x))


def _softplus(x):
    return jnp.maximum(x, 0.0) + jnp.log1p(jnp.exp(-jnp.abs(x)))


def _split_bf16(v, parts):
    out = []
    for _ in range(parts - 1):
        p = v.astype(BF16)
        out.append(p)
        v = v - p.astype(F32)
    out.append(v.astype(BF16))
    return out


def _dot(a, b):
    return jnp.dot(a, b, preferred_element_type=F32)


def _dot_nt(a, b):
    return lax.dot_general(a, b, (((1,), (1,)), ((), ())), preferred_element_type=F32)


def _select_dot(sel, v, parts):
    acc = None
    for p in _split_bf16(v, parts):
        t = _dot(sel, p)
        acc = t if acc is None else acc + t
    return acc


def _rms_normed(x, g):
    ms = jnp.mean(x * x, axis=-1, keepdims=True)
    return (x * lax.rsqrt(ms + EPS) * g).astype(BF16)


def _resident(shape, layer=None, col_block=0):
    if layer is None:
        return pl.BlockSpec(shape, lambda *_: (0,) * len(shape), pipeline_mode=pl.Buffered(1))
    return pl.BlockSpec((None,) + tuple(shape), lambda *_: (layer, 0, col_block), pipeline_mode=pl.Buffered(1))


def _norm_matmul_kernel(x_ref, g_ref, w_ref, wa_ref, o_ref, oa_ref, *, chunk):
    xn = _rms_normed(x_ref[...], g_ref[...])
    oa_ref[...] = _dot(xn, wa_ref[...])
    for c in range(w_ref.shape[1] // chunk):
        sl = slice(c * chunk, (c + 1) * chunk)
        o_ref[:, sl] = _dot(xn, w_ref[:, sl].astype(BF16)).astype(o_ref.dtype)


def _norm_matmul(x, g, w_stack, layer, n, *, tm, chunk, name):
    t, d = x.shape
    assert t % tm == 0 and n % chunk == 0 and n % LANES == 0 and w_stack.shape[2] == n + LANES
    return pl.pallas_call(
        functools.partial(_norm_matmul_kernel, chunk=chunk),
        grid=(t // tm,),
        in_specs=[
            pl.BlockSpec((tm, d), lambda i: (i, 0)),
            _resident((1, d)),
            _resident((d, n), layer),
            _resident((d, LANES), layer, n // LANES),
        ],
        out_specs=[pl.BlockSpec((tm, n), lambda i: (i, 0)), pl.BlockSpec((tm, LANES), lambda i: (i, 0))],
        out_shape=[jax.ShapeDtypeStruct((t, n), BF16), jax.ShapeDtypeStruct((t, LANES), F32)],
        compiler_params=_cparams("parallel"),
        name=name,
    )(x, g.reshape(1, d), w_stack, w_stack)


def _matmul_residual_kernel(*refs, final):
    if final:
        y_ref, w_ref, h_ref, fg_ref, o_ref = refs
    else:
        y_ref, w_ref, h_ref, o_ref = refs
    out = h_ref[...] + _dot(y_ref[...], w_ref[...].astype(BF16))
    if final:
        ms = jnp.mean(out * out, axis=-1, keepdims=True)
        out = out * lax.rsqrt(ms + EPS) * fg_ref[...]
    o_ref[...] = out


def _matmul_residual(y, w_stack, layer, h, final_g=None, *, tm, name, in_place=True):
    t, k = y.shape
    d = w_stack.shape[2]
    assert t % tm == 0 and w_stack.shape[1] == k
    final = final_g is not None
    in_specs = [
        pl.BlockSpec((tm, k), lambda i: (i, 0)),
        _resident((k, d), layer),
        pl.BlockSpec((tm, d), lambda i: (i, 0)),
    ]
    args = [y, w_stack, h]
    if final:
        in_specs.append(_resident((1, d)))
        args.append(final_g.reshape(1, d))
    return pl.pallas_call(
        functools.partial(_matmul_residual_kernel, final=final),
        grid=(t // tm,),
        in_specs=in_specs,
        out_specs=pl.BlockSpec((tm, d), lambda i: (i, 0)),
        out_shape=jax.ShapeDtypeStruct((t, d), F32),
        input_output_aliases={2: 0} if in_place else {},
        compiler_params=_cparams("parallel"),
        name=name,
    )(*args)


FFN_HALO = 16


def _ffn_up_kernel(x_ref, xh_ref, g_ref, w_ref, cw_ref, cb_ref, o_ref, xn_ref, buf_ref, *, tm, f, chunk, seq_len):
    seq_start = (pl.program_id(0) * tm) % seq_len == 0
    xn_ref[0:tm, :] = _rms_normed(x_ref[...], g_ref[...])
    xn_ref[tm:, :] = _rms_normed(xh_ref[...], g_ref[...])
    for c in range(f // chunk):
        sl = slice(c * chunk, (c + 1) * chunk)
        buf = buf_ref.at[c % 2]
        gate = _dot(xn_ref[...], w_ref[:, sl].astype(BF16))
        up = _dot(xn_ref[0:tm, :], w_ref[:, f + c * chunk:f + (c + 1) * chunk].astype(BF16))
        buf[0:SUBLANES, :] = jnp.where(seq_start, 0.0, gate[tm + FFN_HALO - SUBLANES:, :])
        buf[SUBLANES:, :] = gate[0:tm, :]
        conv = cb_ref[:, sl]
        for k in range(FFN_CONV):
            off = SUBLANES - (FFN_CONV - 1) + k
            conv = conv + buf[off:off + tm, :] * cw_ref[k:k + 1, sl]
        sig = 1.0 / (1.0 + jnp.exp2(conv * (-LOG2_E)))
        o_ref[:, sl] = (conv * sig * up).astype(o_ref.dtype)


def _ffn_up(h, g, w_stack, layer, conv_w, conv_b, *, tm, chunk, seq_len, name):
    t, d = h.shape
    f = w_stack.shape[2] // 2
    assert t % tm == 0 and seq_len % tm == 0 and tm % FFN_HALO == 0 and f % chunk == 0
    return pl.pallas_call(
        functools.partial(_ffn_up_kernel, tm=tm, f=f, chunk=chunk, seq_len=seq_len),
        grid=(t // tm,),
        in_specs=[
            pl.BlockSpec((tm, d), lambda i: (i, 0)),
            pl.BlockSpec((FFN_HALO, d), lambda i: (jnp.maximum(i * (tm // FFN_HALO) - 1, 0), 0)),
            _resident((1, d)),
            _resident((d, 2 * f), layer),
            _resident((FFN_CONV, f)),
            _resident((1, f)),
        ],
        out_specs=pl.BlockSpec((tm, f), lambda i: (i, 0)),
        out_shape=jax.ShapeDtypeStruct((t, f), BF16),
        scratch_shapes=[
            pltpu.VMEM((tm + FFN_HALO, d), BF16),
            pltpu.VMEM((2, tm + SUBLANES, chunk), F32),
        ],
        compiler_params=_cparams("parallel"),
        name=name,
    )(h, h, g.reshape(1, d), w_stack, conv_w, conv_b.reshape(1, f))


def _ssd_in_proj_kernel(x_ref, xh_ref, g_ref, w_ref, wa_ref, cw_ref, cb_ref, o_ref, oa_ref, xn_ref, buf_ref,
                        *, tm, d_inner, chunk, seq_len):
    seq_start = (pl.program_id(0) * tm) % seq_len == 0
    xn_ref[0:tm, :] = _rms_normed(x_ref[...], g_ref[...])
    xn_ref[tm:, :] = _rms_normed(xh_ref[...], g_ref[...])
    oa_ref[...] = _dot(xn_ref[0:tm, :], wa_ref[...])
    for c in range(w_ref.shape[1] // chunk):
        sl = slice(c * chunk, (c + 1) * chunk)
        if c * chunk < d_inner:
            pre = _dot(xn_ref[0:tm, :], w_ref[:, sl])
        else:
            csl = slice(c * chunk - d_inner, (c + 1) * chunk - d_inner)
            buf = buf_ref.at[c % 2]
            ext = _dot(xn_ref[...], w_ref[:, sl])
            buf[0:SUBLANES, :] = jnp.where(seq_start, 0.0, ext[tm + FFN_HALO - SUBLANES:, :])
            buf[SUBLANES:, :] = ext[0:tm, :]
            pre = cb_ref[:, csl]
            for k in range(SSD_CONV):
                off = SUBLANES - (SSD_CONV - 1) + k
                pre = pre + buf[off:off + tm, :] * cw_ref[k:k + 1, csl]
        o_ref[:, sl] = (pre / (1.0 + jnp.exp2(pre * (-LOG2_E)))).astype(o_ref.dtype)


def _ssd_in_proj(h, g, w_stack, layer, n, conv_w, conv_b, *, d_inner, tm, chunk, seq_len, name):
    t, d = h.shape
    n_conv = n - d_inner
    assert t % tm == 0 and seq_len % tm == 0 and tm % FFN_HALO == 0 and w_stack.shape[2] == n + LANES
    assert n % chunk == 0 and d_inner % chunk == 0 and conv_w.shape == (SSD_CONV, n_conv)
    return pl.pallas_call(
        functools.partial(_ssd_in_proj_kernel, tm=tm, d_inner=d_inner, chunk=chunk, seq_len=seq_len),
        grid=(t // tm,),
        in_specs=[
            pl.BlockSpec((tm, d), lambda i: (i, 0)),
            pl.BlockSpec((FFN_HALO, d), lambda i: (jnp.maximum(i * (tm // FFN_HALO) - 1, 0), 0)),
            _resident((1, d)),
            _resident((d, n), layer),
            _resident((d, LANES), layer, n // LANES),
            _resident((SSD_CONV, n_conv)),
            _resident((1, n_conv)),
        ],
        out_specs=[pl.BlockSpec((tm, n), lambda i: (i, 0)), pl.BlockSpec((tm, LANES), lambda i: (i, 0))],
        out_shape=[jax.ShapeDtypeStruct((t, n), BF16), jax.ShapeDtypeStruct((t, LANES), F32)],
        scratch_shapes=[
            pltpu.VMEM((tm + FFN_HALO, d), BF16),
            pltpu.VMEM((2, tm + SUBLANES, chunk), F32),
        ],
        compiler_params=_cparams("parallel"),
        name=name,
    )(h, h, g.reshape(1, d), w_stack, w_stack, conv_w, conv_b.reshape(1, n_conv))


def _ssd_kernel(z_ref, x_ref, bc_ref, dt_ref, dtb_ref, alog_ref, dsk_ref, ng_ref,
                e_ref, tri_ref, o_ref, state_ref, *, ts):
    d_inner = x_ref.shape[1]
    gw = d_inner // SSD_GROUPS
    gn = SSD_GROUPS * SSD_STATE

    @pl.when(pl.program_id(1) == 0)
    def _():
        state_ref[...] = jnp.zeros_like(state_ref)

    dt_all = _softplus(dt_ref[...] + dtb_ref[...])
    da_all = dt_all * (-jnp.exp(alog_ref[...]))

    ll = SSD_CHUNK
    row = lax.broadcasted_iota(jnp.int32, (ll, ll), 0)
    col = lax.broadcasted_iota(jnp.int32, (ll, ll), 1)
    causal = row >= col
    first_head = lax.broadcasted_iota(jnp.int32, (ll, LANES), 1) < HEAD_DIM
    expand = e_ref[...]

    for c in range(ts // ll):
        r0 = c * ll
        xs = x_ref[r0:r0 + ll, :].astype(F32)
        bc = bc_ref[r0:r0 + ll, :]
        dt = dt_all[r0:r0 + ll]
        acs = _select_dot(tri_ref[...], da_all[r0:r0 + ll], 3)
        acs_t = acs.T
        tot = acs[ll - 1:ll, :]
        tile16 = (2 * SUBLANES, LANES)
        per_head = jnp.concatenate(
            [jnp.exp(acs).astype(BF16), dt.astype(BF16), jnp.exp(tot - acs).astype(BF16)]
            + [jnp.broadcast_to(p, tile16) for p in _split_bf16(jnp.exp(tot), 3)], axis=0)
        per_chan = _dot(per_head, expand)
        ea_x = per_chan[0:ll]
        dt_x = per_chan[ll:2 * ll]
        ds_x = per_chan[2 * ll:3 * ll]
        r1 = 3 * ll
        chunk_decay_x = (per_chan[r1:r1 + 1] + per_chan[r1 + 16:r1 + 17]) + per_chan[r1 + 32:r1 + 33]
        xdt = xs * dt_x
        x_state = (xdt * ds_x).astype(BF16)

        ys = []
        for g in range(SSD_GROUPS):
            b_g = bc[:, g * SSD_STATE:(g + 1) * SSD_STATE]
            c_g = bc[:, gn + g * SSD_STATE:gn + (g + 1) * SSD_STATE]
            cb = _dot_nt(c_g, b_g)
            st = state_ref[g]
            y_off = _dot(c_g, st.astype(BF16)) * ea_x[:, g * gw:(g + 1) * gw]
            y_diag = []
            for pr in range(gw // LANES):
                h0 = (g * gw + pr * LANES) // HEAD_DIM
                ms = []
                for hh in (h0, h0 + 1):
                    diff = acs[:, hh:hh + 1] - acs_t[hh:hh + 1, :]
                    dec = jnp.exp(jnp.where(causal, diff, -jnp.inf))
                    ms.append((cb * dec).astype(BF16))
                lhs = jnp.concatenate(ms, axis=1)
                xp = xdt[:, g * gw + pr * LANES:g * gw + (pr + 1) * LANES]
                rhs = jnp.concatenate([jnp.where(first_head, xp, 0.0),
                                       jnp.where(first_head, 0.0, xp)], axis=0).astype(BF16)
                y_diag.append(_dot(lhs, rhs))
            ys.append(jnp.concatenate(y_diag, axis=1) + y_off)
            new = _dot(b_g.astype(F32).T.astype(BF16), x_state[:, g * gw:(g + 1) * gw])
            state_ref[g] = st * chunk_decay_x[:, g * gw:(g + 1) * gw] + new

        y = jnp.concatenate(ys, axis=1) + xs * dsk_ref[...]
        yz = y * z_ref[r0:r0 + ll, :].astype(F32)
        outs = []
        for g in range(SSD_GROUPS):
            blk = yz[:, g * gw:(g + 1) * gw]
            ms = jnp.mean(blk * blk, axis=-1, keepdims=True)
            outs.append(blk * lax.rsqrt(ms + EPS))
        o_ref[r0:r0 + ll, :] = (jnp.concatenate(outs, axis=1) * ng_ref[...]).astype(o_ref.dtype)


def _ssd_core(proj, dt_raw, dt_bias, a_log, d_skip, norm_g, *, batch, seq_len, ts, name):
    t = proj.shape[0]
    n_heads = d_skip.shape[0]
    d_inner = n_heads * HEAD_DIM
    gn = SSD_GROUPS * SSD_STATE
    assert proj.shape[1] == 2 * d_inner + 2 * gn and (2 * d_inner) % (2 * gn) == 0
    assert seq_len % ts == 0 and ts % SSD_CHUNK == 0 and n_heads <= LANES
    ns = seq_len // ts
    bc_blk = 2 * d_inner // (2 * gn)

    def pad_lanes(v):
        return jnp.pad(v.astype(F32), (0, LANES - v.shape[0])).reshape(1, LANES)

    expand = np.zeros((LANES, d_inner), np.float32)
    expand[np.arange(d_inner) // HEAD_DIM, np.arange(d_inner)] = 1.0
    tri = np.tril(np.ones((SSD_CHUNK, SSD_CHUNK), np.float32))

    def rows(b, s):
        return b * ns + s

    return pl.pallas_call(
        functools.partial(_ssd_kernel, ts=ts),
        grid=(batch, ns),
        in_specs=[
            pl.BlockSpec((ts, d_inner), lambda b, s: (rows(b, s), 0)),
            pl.BlockSpec((ts, d_inner), lambda b, s: (rows(b, s), 1)),
            pl.BlockSpec((ts, 2 * gn), lambda b, s: (rows(b, s), bc_blk)),
            pl.BlockSpec((ts, LANES), lambda b, s: (rows(b, s), 0)),
            _resident((1, LANES)),
            _resident((1, LANES)),
            _resident((1, d_inner)),
            _resident((1, d_inner)),
            _resident((LANES, d_inner)),
            _resident((SSD_CHUNK, SSD_CHUNK)),
        ],
        out_specs=pl.BlockSpec((ts, d_inner), lambda b, s: (rows(b, s), 0)),
        out_shape=jax.ShapeDtypeStruct((t, d_inner), BF16),
        scratch_shapes=[pltpu.VMEM((SSD_GROUPS, SSD_STATE, d_inner // SSD_GROUPS), F32)],
        compiler_params=_cparams("parallel", "arbitrary"),
        name=name,
    )(proj, proj, proj, dt_raw,
      pad_lanes(dt_bias), pad_lanes(a_log),
      jnp.repeat(d_skip, HEAD_DIM).reshape(1, d_inner), norm_g.reshape(1, d_inner),
      jnp.asarray(expand, BF16), jnp.asarray(tri, BF16))


GATE_PARTS = 3
GATE_LANES = 2 * GATE_PARTS
VT_ROWS = LANES + 16
KEY_BLOCKS_PER_QUERY_BLOCK = 4
CUM_BLOCK = 256


def _fox_prep_kernel(qk_ref, v_ref, fl_ref, bf_ref, qg_ref, kg_ref, hs_ref, tri_ref, pq_ref, pk_ref,
                     qt_ref, ok_ref, vt_ref, carry_ref, *, ts, tq, tk, d, scale):
    @pl.when(pl.program_id(1) == 0)
    def _():
        carry_ref[...] = jnp.zeros_like(carry_ref)

    for j in range(d // LANES):
        for kk in range(ts // tk):
            vj = v_ref[kk * tk:(kk + 1) * tk, j * LANES:(j + 1) * LANES]
            vt_ref[j, kk, 0:LANES, :] = vj.T
            vt_ref[j, kk, LANES:, :] = jnp.ones((VT_ROWS - LANES, tk), BF16)

    logit = fl_ref[...] + bf_ref[...]
    log_f = jnp.minimum(logit, 0.0) - jnp.log1p(jnp.exp(-jnp.abs(logit)))
    carry = carry_ref[...]
    cum_blocks = []
    for r in range(ts // CUM_BLOCK):
        c = _select_dot(tri_ref[...], log_f[r * CUM_BLOCK:(r + 1) * CUM_BLOCK], 3) + carry
        cum_blocks.append(c)
        carry = c[CUM_BLOCK - 1:CUM_BLOCK, :]
    carry_ref[...] = carry
    cum = jnp.concatenate(cum_blocks, axis=0)
    pieces = _split_bf16(cum * LOG2_E, GATE_PARTS) + [jnp.ones((ts, LANES), BF16)]
    feats = jnp.concatenate(pieces, axis=1)
    gq_t = _dot(feats, pq_ref[...]).astype(BF16).T
    gk = _dot(feats, pk_ref[...]).astype(BF16)

    def head_norm(base, jj, g_ref, mult):
        sl = slice(2 * jj * LANES, (2 * jj + 2) * LANES)
        v = qk_ref[:, base + 2 * jj * LANES:base + (2 * jj + 2) * LANES].astype(F32)
        ssq = _dot((v * v).astype(BF16), hs_ref[...])
        vn = v * lax.rsqrt(ssq * (1.0 / HEAD_DIM) + EPS) * (g_ref[:, sl] * mult)
        return vn.astype(BF16)

    for jj in range(d // (2 * LANES)):
        kn = head_norm(d, jj, kg_ref, 1.0)
        qn = head_norm(0, jj, qg_ref, scale)
        for half in range(2):
            j = 2 * jj + half
            hl = slice(half * LANES, (half + 1) * LANES)
            ok_ref[:, 2 * j * LANES:(2 * j + 1) * LANES] = kn[:, hl]
            ok_ref[:, (2 * j + 1) * LANES:(2 * j + 2) * LANES] = gk
            qn_t = qn[:, hl].astype(F32).T.astype(BF16)
            qt_ref[j, 0:LANES, :] = qn_t
            qt_ref[j, LANES:, :] = gq_t


def _gate_placements(n_heads):
    assert n_heads * GATE_LANES <= LANES
    pq = np.zeros((4 * LANES, LANES), np.float32)
    pk = np.zeros((4 * LANES, LANES), np.float32)
    ones_row = GATE_PARTS * LANES
    for head in range(n_heads):
        base = head * GATE_LANES
        for p in range(GATE_PARTS):
            pq[p * LANES + head, base + p] = 1.0
            pk[ones_row, base + p] = 1.0
            pq[ones_row, base + GATE_PARTS + p] = 1.0
            pk[p * LANES + head, base + GATE_PARTS + p] = -1.0
    return pq, pk


def _fox_prep(proj, flog, b_f, qg, kg, *, batch, seq_len, ts, tq, tk, name):
    t = proj.shape[0]
    n_heads = b_f.shape[0]
    d = n_heads * HEAD_DIM
    assert seq_len % ts == 0 and ts % tk == 0 and tq % ts == 0 and n_heads <= LANES
    assert ts % CUM_BLOCK == 0 and d % (2 * LANES) == 0
    ns = seq_len // ts
    n_pairs = d // LANES
    pq, pk = _gate_placements(n_heads)
    hs = np.kron(np.eye(2 * LANES // HEAD_DIM, dtype=np.float32), np.ones((HEAD_DIM, HEAD_DIM), np.float32))
    tri = np.tril(np.ones((CUM_BLOCK, CUM_BLOCK), np.float32))
    const = lambda b, s: (0, 0)
    rows = lambda b, s: (b * ns + s, 0)
    return pl.pallas_call(
        functools.partial(_fox_prep_kernel, ts=ts, tq=tq, tk=tk, d=d, scale=HEAD_DIM ** -0.5 * LOG2_E),
        grid=(batch, ns),
        in_specs=[
            pl.BlockSpec((ts, 2 * d), rows),
            pl.BlockSpec((ts, d), lambda b, s: (b * ns + s, 2)),
            pl.BlockSpec((ts, LANES), rows),
            pl.BlockSpec((1, LANES), const),
            pl.BlockSpec((1, d), const),
            pl.BlockSpec((1, d), const),
            pl.BlockSpec((2 * LANES, 2 * LANES), const),
            pl.BlockSpec((CUM_BLOCK, CUM_BLOCK), const),
            pl.BlockSpec((4 * LANES, LANES), const),
            pl.BlockSpec((4 * LANES, LANES), const),
        ],
        out_specs=[
            pl.BlockSpec((None, n_pairs, None, 2 * LANES, ts),
                         lambda b, s: (b, 0, s // (tq // ts), 0, s % (tq // ts))),
            pl.BlockSpec((ts, 2 * d), rows),
            pl.BlockSpec((None, n_pairs, ts // tk, VT_ROWS, tk), lambda b, s: (b, 0, s, 0, 0)),
        ],
        out_shape=[
            jax.ShapeDtypeStruct((batch, n_pairs, seq_len // tq, 2 * LANES, tq), BF16),
            jax.ShapeDtypeStruct((t, 2 * d), BF16),
            jax.ShapeDtypeStruct((batch, n_pairs, seq_len // tk, VT_ROWS, tk), BF16),
        ],
        scratch_shapes=[pltpu.VMEM((1, LANES), F32)],
        compiler_params=_cparams("parallel", "arbitrary"),
        name=name,
    )(proj, proj, flog,
      jnp.pad(b_f.astype(F32), (0, LANES - n_heads)).reshape(1, LANES),
      jnp.tile(qg, n_heads).reshape(1, d), jnp.tile(kg, n_heads).reshape(1, d),
      jnp.asarray(hs, BF16), jnp.asarray(tri, BF16), jnp.asarray(pq, BF16), jnp.asarray(pk, BF16))


def _fox_attn_kernel(qt_ref, k_ref, vt_ref, gate_ref, o_ref, m_ref, acc_ref, s_ref, smax_ref, *, tq, tk):
    qi = pl.program_id(2)
    lane = lax.broadcasted_iota(jnp.int32, (2 * LANES, 1), 0)
    g0 = LANES + 2 * GATE_LANES * pl.program_id(1)
    head_lanes = (
        (lane < HEAD_DIM) | ((lane >= g0) & (lane < g0 + GATE_LANES)),
        ((lane >= HEAD_DIM) & (lane < LANES)) | ((lane >= g0 + GATE_LANES) & (lane < g0 + 2 * GATE_LANES)),
    )
    qt = qt_ref[...]
    q2t = jnp.concatenate([jnp.where(sel, qt, jnp.zeros_like(qt)) for sel in head_lanes], axis=1)

    def key_block(j):
        return k_ref[pl.ds(pl.multiple_of(j * tk, tk), tk), :]

    def scores(slot, j):
        s = _dot(key_block(j), q2t)
        s_ref[slot] = s
        smax_ref[slot] = jnp.broadcast_to(jnp.max(s, axis=0, keepdims=True), smax_ref.shape[1:])

    def softmax_step(slot, cols, s, s_max, j):
        m_prev = m_ref[slot, :, cols]
        m_new = jnp.maximum(m_prev, s_max)
        m_safe = jnp.where(m_new == -jnp.inf, 0.0, m_new)
        p = jnp.exp2(s - m_safe[0:1, :])
        alpha = jnp.exp2(m_prev - m_safe)
        acc_ref[slot, :, cols] = alpha[0:1, :] * acc_ref[slot, :, cols] + _dot(vt_ref[j], p.astype(BF16))
        m_ref[slot, :, cols] = m_new

    def update(slot, j):
        softmax_step(slot, slice(0, 2 * tq), s_ref[slot], smax_ref[slot], j)

    def diag_scores(slot, j, d):
        w = tq - d * tk
        rhs = q2t if d == 0 else jnp.concatenate([q2t[:, d * tk:tq], q2t[:, tq + d * tk:]], axis=1)
        s_ref[slot, :, 0:2 * w] = _dot(key_block(j), rhs)

    def diag_update(slot, j, d):
        w = tq - d * tk
        key = lax.broadcasted_iota(jnp.int32, (tk, tk), 0)
        qry = lax.broadcasted_iota(jnp.int32, (tk, tk), 1)
        for hd in range(2):
            s = s_ref[slot, :, hd * w:(hd + 1) * w]
            tri = jnp.where(qry >= key, s[:, 0:tk], -jnp.inf)
            s = tri if w == tk else jnp.concatenate([tri, s[:, tk:]], axis=1)
            softmax_step(slot, slice(hd * tq + d * tk, (hd + 1) * tq), s,
                         jnp.max(s, axis=0, keepdims=True), j)

    scores(0, 0)
    m_ref[...] = jnp.full_like(m_ref, -jnp.inf)
    acc_ref[...] = jnp.zeros_like(acc_ref)

    def pair(i):
        scores(1, 2 * i + 1)
        update(0, 2 * i)
        scores(0, 2 * i + 2)
        update(1, 2 * i + 1)

    def two_pairs(i, carry):
        pair(2 * i)
        pair(2 * i + 1)
        return carry

    nkq = KEY_BLOCKS_PER_QUERY_BLOCK
    assert nkq % 4 == 0 and tq == nkq * tk
    lax.fori_loop(0, (nkq // 4) * qi, two_pairs, 0)
    j0 = nkq * qi
    for d in range(nkq):
        if d + 1 < nkq:
            diag_scores((d + 1) % 2, j0 + d + 1, d + 1)
        diag_update(d % 2, j0 + d, d)

    m = jnp.maximum(m_ref[0], m_ref[1])
    w0 = jnp.exp2(m_ref[0] - m)
    w1 = jnp.exp2(m_ref[1] - m)
    acc = w0[0:1, :] * acc_ref[0] + w1[0:1, :] * acc_ref[1]
    o_t = jnp.concatenate([acc[0:HEAD_DIM, 0:tq] / acc[LANES:LANES + 1, 0:tq],
                           acc[HEAD_DIM:LANES, tq:] / acc[LANES:LANES + 1, tq:]], axis=0)
    o_ref[...] = (o_t.T * _sigmoid(gate_ref[...].astype(F32))).astype(o_ref.dtype)


def _fox_attn(qt, kp, vt, proj, *, batch, seq_len, tq, tk, name):
    t = kp.shape[0]
    d = kp.shape[1] // 2
    n_pairs = d // LANES
    nq = seq_len // tq
    nk = seq_len // tk
    assert seq_len % tq == 0 and tq == KEY_BLOCKS_PER_QUERY_BLOCK * tk and tk % LANES == 0
    assert qt.shape == (batch, n_pairs, nq, 2 * LANES, tq) and vt.shape == (batch, n_pairs, nk, VT_ROWS, tk)
    g_blk = 3 * d // LANES
    return pl.pallas_call(
        functools.partial(_fox_attn_kernel, tq=tq, tk=tk),
        grid=(batch, n_pairs, nq),
        in_specs=[
            pl.BlockSpec((None, None, None, 2 * LANES, tq), lambda b, p, i: (b, p, i, 0, 0)),
            pl.BlockSpec((seq_len, 2 * LANES), lambda b, p, i: (b, p)),
            pl.BlockSpec((None, None, nk, VT_ROWS, tk), lambda b, p, i: (b, p, 0, 0, 0)),
            pl.BlockSpec((tq, LANES), lambda b, p, i: (b * nq + i, g_blk + p)),
        ],
        out_specs=pl.BlockSpec((tq, LANES), lambda b, p, i: (b * nq + i, p)),
        out_shape=jax.ShapeDtypeStruct((t, d), BF16),
        scratch_shapes=[
            pltpu.VMEM((2, SUBLANES, 2 * tq), F32),
            pltpu.VMEM((2, VT_ROWS, 2 * tq), F32),
            pltpu.VMEM((2, tk, 2 * tq), F32),
            pltpu.VMEM((2, SUBLANES, 2 * tq), F32),
        ],
        compiler_params=_cparams("parallel", "parallel", "arbitrary"),
        name=name,
    )(qt, kp, vt, proj)


def _tile(n, pref):
    tile = min(n, pref)
    while n % tile:
        tile //= 2
    return tile


def kernel(x, mix_norm_g, ffn_norm_g, ssd_w_in, ssd_conv_w, ssd_conv_b, ssd_dt_bias, ssd_a_log, ssd_d,
           ssd_norm_g, ssd_w_out, fox_w_in, fox_b_f, fox_q_norm_g, fox_k_norm_g, fox_w_out,
           ffn_w_up, ffn_conv_w, ffn_conv_b, ffn_w_down, final_norm_g):
    batch, seq_len, d_model = x.shape
    t = batch * seq_len
    depth = mix_norm_g.shape[0]
    tm_proj = _tile(seq_len, 512)
    tm_ffn = _tile(seq_len, 1024)
    tm_out = _tile(seq_len, 1024)
    ts_ssd = _tile(seq_len, 512)
    tq = _tile(seq_len, 1024)
    ts_prep = tq

    def padded_bf16(w_stack, n_main):
        layers, d_in, width = w_stack.shape
        assert 0 < width - n_main <= LANES
        return jnp.zeros((layers, d_in, n_main + LANES), BF16).at[:, :, :width].set(w_stack.astype(BF16))

    n_ssd_main = ssd_w_in.shape[2] - ssd_d.shape[1]
    n_fox_main = fox_w_in.shape[2] - fox_b_f.shape[1]
    ssd_w_in_b = padded_bf16(ssd_w_in, n_ssd_main)
    fox_w_in_b = padded_bf16(fox_w_in, n_fox_main)
    tk = tq // KEY_BLOCKS_PER_QUERY_BLOCK

    h = x.reshape(t, d_model)
    for i in range(depth):
        j = i // 2
        if i % 2 == 0:
            proj, dt_raw = _ssd_in_proj(
                h, mix_norm_g[i], ssd_w_in_b, j, n_ssd_main, ssd_conv_w[j], ssd_conv_b[j],
                d_inner=ssd_d.shape[1] * HEAD_DIM, tm=tm_proj, chunk=512, seq_len=seq_len,
                name=f"ssd_in_proj_{j}")
            y = _ssd_core(proj, dt_raw, ssd_dt_bias[j], ssd_a_log[j], ssd_d[j], ssd_norm_g[j],
                          batch=batch, seq_len=seq_len, ts=ts_ssd, name=f"ssd_core_{j}")
            h = _matmul_residual(y, ssd_w_out, j, h, tm=tm_out, name=f"ssd_out_proj_{j}", in_place=i > 0)
        else:
            proj, flog = _norm_matmul(
                h, mix_norm_g[i], fox_w_in_b, j, n_fox_main, tm=tm_proj, chunk=512, name=f"fox_in_proj_{j}")
            qt, kp, vt = _fox_prep(proj, flog, fox_b_f[j], fox_q_norm_g[j], fox_k_norm_g[j],
                                   batch=batch, seq_len=seq_len, ts=ts_prep, tq=tq, tk=tk, name=f"fox_prep_{j}")
            o = _fox_attn(qt, kp, vt, proj, batch=batch, seq_len=seq_len, tq=tq, tk=tk, name=f"fox_attn_{j}")
            h = _matmul_residual(o, fox_w_out, j, h, tm=tm_out, name=f"fox_out_proj_{j}")
        act = _ffn_up(h, ffn_norm_g[i], ffn_w_up, i, ffn_conv_w[i], ffn_conv_b[i],
                      tm=tm_ffn, chunk=256, seq_len=seq_len, name=f"ffn_up_{i}")
        h = _matmul_residual(act, ffn_w_down, i, h, final_norm_g if i == depth - 1 else None,
                             tm=tm_out, name=f"ffn_down_{i}")
    return h.reshape(batch, seq_len, d_model)
```
